```python
import math
import jax
import jax.numpy as jnp
from jax import lax
import numpy as np

D_MODEL = 1024
BATCH = 4
SEQ = 8192
DEPTH = 2
DEC_BATCH = 32
DEC_SEQ = 4
PAST_LEN = 16384
PAGE_SIZE = 128

N_A_LAYERS = DEPTH // 2
N_B_LAYERS = DEPTH - N_A_LAYERS
HEAD_DIM = 128
DN_HEADS = D_MODEL // HEAD_DIM
DN_KEY_DIM = HEAD_DIM
DN_VAL_DIM = HEAD_DIM
DN_QK_WIDTH = DN_HEADS * DN_KEY_DIM
DN_V_WIDTH = DN_HEADS * DN_VAL_DIM
DN_CONV = 4
DN_CONV_DIM = 2 * DN_QK_WIDTH + DN_V_WIDTH
DN_PROJ = DN_CONV_DIM + DN_V_WIDTH + 2 * DN_HEADS
DN_CHUNK = 64
SB_HEADS = D_MODEL // HEAD_DIM
SB_WIDTH = SB_HEADS * HEAD_DIM
SB_BIAS_INIT = -8.0
Q_BLOCK = 128
FFN_HIDDEN = -(-8 * D_MODEL // (3 * 256)) * 256
NORM_EPS = 1e-6
L2_EPS = 1e-6

kernel_name = 'yoco_deltanet_stickbreaking_step'


def rmsnorm(x, g):
    xf = x.astype(jnp.float32)
    return xf * lax.rsqrt(jnp.mean(xf * xf, axis=-1, keepdims=True) + NORM_EPS) * g.astype(jnp.float32)


def l2norm(x):
    xf = x.astype(jnp.float32)
    return xf * lax.rsqrt(jnp.sum(xf * xf, axis=-1, keepdims=True) + L2_EPS)


def adaln(c, w, b, n):
    m = jax.nn.silu(c.astype(jnp.float32)) @ w + b
    return jnp.split(m[:, None, :], n, axis=-1)


def causal_conv(x, buf, w):
    L = x.shape[1]
    xx = jnp.concatenate([buf.astype(x.dtype), x], axis=1)
    y = sum(xx[:, i:i + L] * w[i] for i in range(DN_CONV))
    return jax.nn.silu(y), xx[:, L:]


def gated_delta_rule(q, k, v, beta, g, s0):
    b, L, h, _ = q.shape
    dv = v.shape[-1]
    n = -(-L // DN_CHUNK)
    pad = n * DN_CHUNK - L

    def blocks(t):
        t = jnp.pad(t.astype(jnp.float32), [(0, 0), (0, pad)] + [(0, 0)] * (t.ndim - 2))
        t = t.reshape((b, n, DN_CHUNK) + t.shape[2:])
        return jnp.moveaxis(t, 3, 1)

    q, k, v, beta, g = (blocks(t) for t in (q, k, v, beta, g))
    G = lax.cumsum(g, axis=3)
    idx = jnp.arange(DN_CHUNK)
    incl = idx[:, None] >= idx[None, :]
    strict = idx[:, None] > idx[None, :]
    decay = jnp.exp(jnp.where(incl, G[..., :, None] - G[..., None, :], -jnp.inf))
    kb = k * beta[..., None]
    a = jnp.einsum('bhnik,bhnjk->bhnij', kb, k) * jnp.where(strict, decay, 0.0)
    eye = jnp.eye(DN_CHUNK, dtype=jnp.float32)
    t_inv = lax.linalg.triangular_solve(eye + a, jnp.broadcast_to(eye, a.shape), left_side=True,
                                        lower=True, unit_diagonal=True)
    u = t_inv @ (v * beta[..., None])
    w = t_inv @ (kb * jnp.exp(G)[..., None])
    qk = jnp.einsum('bhnik,bhnjk->bhnij', q, k) * decay
    q_dec = q * jnp.exp(G)[..., None]
    k_dec = k * jnp.exp(G[..., -1:] - G)[..., None]
    g_end = jnp.exp(G[..., -1])

    def step(s, xs):
        u_i, w_i, qk_i, qd_i, kd_i, ge_i = xs
        v_new = u_i - w_i @ s
        o_i = qd_i @ s + qk_i @ v_new
        s = s * ge_i[..., None, None] + jnp.swapaxes(kd_i, -1, -2) @ v_new
        return s, o_i

    xs = tuple(jnp.moveaxis(t, 2, 0) for t in (u, w, qk, q_dec, k_dec, g_end))
    s, o = lax.scan(step, s0.astype(jnp.float32), xs)
    o = jnp.moveaxis(jnp.moveaxis(o, 0, 2), 1, 3).reshape(b, n * DN_CHUNK, h, dv)[:, :L]
    return o, s


def delta_mixer(h, conv_buf, s0, w_in, conv_w, a_log, dt_bias, onorm_g, w_out):
    b, L, _ = h.shape
    proj = h @ w_in
    qkv, z, beta_logit, a_logit = jnp.split(
        proj, [DN_CONV_DIM, DN_CONV_DIM + DN_V_WIDTH, DN_CONV_DIM + DN_V_WIDTH + DN_HEADS], axis=-1)
    qkv, new_buf = causal_conv(qkv, conv_buf, conv_w)
    q, k, v = jnp.split(qkv, [DN_QK_WIDTH, 2 * DN_QK_WIDTH], axis=-1)
    q = l2norm(q.reshape(b, L, DN_HEADS, DN_KEY_DIM)) * DN_KEY_DIM ** -0.5
    k = l2norm(k.reshape(b, L, DN_HEADS, DN_KEY_DIM))
    v = v.reshape(b, L, DN_HEADS, DN_VAL_DIM)
    beta = jax.nn.sigmoid(beta_logit.astype(jnp.float32))
    g = -jnp.exp(a_log.astype(jnp.float32)) * jax.nn.softplus(a_logit.astype(jnp.float32) + dt_bias)
    o, s = gated_delta_rule(q, k, v, beta, g, s0)
    o = rmsnorm(o, onorm_g) * jax.nn.silu(z.reshape(b, L, DN_HEADS, DN_VAL_DIM).astype(jnp.float32))
    return o.reshape(b, L, DN_V_WIDTH) @ w_out, new_buf, s


def stick_breaking(q, segs, q_pos, seg_pos, logit_bias):
    b, lq, h, dh = q.shape
    qb = min(Q_BLOCK, lq)
    nb = -(-lq // qb)
    pad = nb * qb - lq
    q = jnp.pad(q, ((0, 0), (0, pad), (0, 0), (0, 0)))
    q_pos = jnp.pad(q_pos, (0, pad), constant_values=-1)
    q_blocks = jnp.moveaxis(q.reshape(b, nb, qb, h, dh), 1, 0)
    p_blocks = q_pos.reshape(nb, qb)
    k_pos = jnp.concatenate(seg_pos)
    scale = dh ** -0.5
    bias = logit_bias.astype(jnp.float32)[None, :, None, None]

    def block(args):
        qq, pp = args
        z = jnp.concatenate([jnp.einsum('bqhd,bkhd->bhqk', qq, kk) for kk, _ in segs],
                            axis=-1).astype(jnp.float32) * scale + bias
        mask = k_pos[None, :] < pp[:, None]
        log_fail = jnp.where(mask, jax.nn.log_sigmoid(-z), 0.0)
        after = lax.cumsum(log_fail, axis=3, reverse=True) - log_fail
        att = jnp.exp(jnp.where(mask, jax.nn.log_sigmoid(z) + after, -jnp.inf))
        outs = []
        off = 0
        for kk, vv in segs:
            nk = kk.shape[1]
            outs.append(jnp.einsum('bhqk,bkhd->bqhd', att[..., off:off + nk], vv.astype(jnp.float32)))
            off += nk
        return sum(outs[1:], outs[0])

    o = lax.map(block, (q_blocks, p_blocks))
    return jnp.moveaxis(o, 0, 1).reshape(b, nb * qb, h, dh)[:, :lq]


def swiglu(h, w_in, w_out):
    gate, up = jnp.split(h @ w_in, 2, axis=-1)
    return (jax.nn.silu(gate) * up) @ w_out


def trunk(x, c, dn_bufs, dn_states, k_past, v_past, p):
    b, L, _ = x.shape
    past = 0 if k_past is None else k_past.shape[1]
    q_pos = past + jnp.arange(L)
    new_bufs, new_states = [], []
    k_new = v_new = None
    for i in range(DEPTH):
        sh1, sc1, gt1, sh2, sc2, gt2 = adaln(c, p['ada_w'][i], p['ada_b'][i], 6)
        h = rmsnorm(x, p['mix_pre_g'][i]) * (1.0 + sc1) + sh1
        if i < N_A_LAYERS:
            out, buf, s = delta_mixer(h, dn_bufs[i], dn_states[i], p['dn_w_in'][i], p['dn_conv_w'][i],
                                      p['dn_a_log'][i], p['dn_dt_bias'][i], p['dn_onorm_g'][i], p['dn_w_out'][i])
            new_bufs.append(buf)
            new_states.append(s)
        else:
            j = i - N_A_LAYERS
            if j == 0:
                ksh, ksc = adaln(c, p['kv_ada_w'], p['kv_ada_b'], 2)
                hk = rmsnorm(x, p['kv_norm_g']) * (1.0 + ksc) + ksh
                k_new, v_new = [t.reshape(b, L, SB_HEADS, HEAD_DIM)
                                for t in jnp.split(hk @ p['sb_w_kv'], 2, axis=-1)]
            q = (h @ p['sb_w_q'][j]).reshape(b, L, SB_HEADS, HEAD_DIM)
            segs = [(k_new, v_new)]
            seg_pos = [q_pos]
            if k_past is not None:
                segs = [(k_past, v_past)] + segs
                seg_pos = [jnp.arange(past)] + seg_pos
            o = stick_breaking(q, segs, q_pos, seg_pos, p['sb_logit_bias'][j])
            out = o.reshape(b, L, SB_WIDTH) @ p['sb_w_o'][j]
        x = x + (gt1 * rmsnorm(out, p['mix_post_g'][i])).astype(x.dtype)
        h = rmsnorm(x, p['ffn_pre_g'][i]) * (1.0 + sc2) + sh2
        out = swiglu(h, p['ffn_w_in'][i], p['ffn_w_out'][i])
        x = x + (gt2 * rmsnorm(out, p['ffn_post_g'][i])).astype(x.dtype)
    return x, jnp.stack(new_bufs), jnp.stack(new_states), k_new, v_new


def setup_inputs(seed: int = 0) -> dict:
    key = jax.random.key(seed)
    ks = iter(jax.random.split(key, 40))
    D = D_MODEL

    def nrm(shape, scale):
        return jax.random.normal(next(ks), shape, jnp.float32) * scale

    def gain(shape):
        return 1.0 + nrm(shape, 0.05)

    n_pages = PAST_LEN // PAGE_SIZE
    used = DEC_BATCH * n_pages
    n_pool = used + max(1, used // 4)
    page_table = jax.random.permutation(next(ks), n_pool)[:used].reshape(DEC_BATCH, n_pages).astype(jnp.int32)
    dt = jnp.exp(jax.random.uniform(next(ks), (N_A_LAYERS, DN_HEADS), jnp.float32,
                                    math.log(1e-3), math.log(1e-1)))
    dt_bias = dt + jnp.log(-jnp.expm1(-dt))
    a_log = jnp.log(jax.random.uniform(next(ks), (N_A_LAYERS, DN_HEADS), jnp.float32, 1.0, 16.0))
    return {
        'x_prompt': nrm((BATCH, SEQ, D), 1.0),
        'x_sample': nrm((DEC_BATCH, DEC_SEQ, D), 1.0),
        'c_prompt': nrm((BATCH, D), 1.0),
        'c_sample': nrm((DEC_BATCH, D), 1.0),
        'state_dn_conv': nrm((N_A_LAYERS, DEC_BATCH, DN_CONV - 1, DN_CONV_DIM), 1.0),
        'state_dn_S': nrm((N_A_LAYERS, DEC_BATCH, DN_HEADS, DN_KEY_DIM, DN_VAL_DIM), 0.05),
        'cache_k': nrm((n_pool, PAGE_SIZE, SB_HEADS, HEAD_DIM), 1.0),
        'cache_v': nrm((n_pool, PAGE_SIZE, SB_HEADS, HEAD_DIM), 1.0),
        'page_table': page_table,
        'ada_w': nrm((DEPTH, D, 6 * D), D ** -0.5),
        'ada_b': nrm((DEPTH, 6 * D), 0.02),
        'mix_pre_g': gain((DEPTH, D)),
        'mix_post_g': gain((DEPTH, D)),
        'ffn_pre_g': gain((DEPTH, D)),
        'ffn_post_g': gain((DEPTH, D)),
        'ffn_w_in': nrm((DEPTH, D, 2 * FFN_HIDDEN), D ** -0.5),
        'ffn_w_out': nrm((DEPTH, FFN_HIDDEN, D), FFN_HIDDEN ** -0.5),
        'dn_w_in': nrm((N_A_LAYERS, D, DN_PROJ), D ** -0.5),
        'dn_conv_w': nrm((N_A_LAYERS, DN_CONV, DN_CONV_DIM), DN_CONV ** -0.5),
        'dn_a_log': a_log,
        'dn_dt_bias': dt_bias,
        'dn_onorm_g': gain((N_A_LAYERS, DN_VAL_DIM)),
        'dn_w_out': nrm((N_A_LAYERS, DN_V_WIDTH, D), DN_V_WIDTH ** -0.5),
        'kv_ada_w': nrm((D, 2 * D), D ** -0.5),
        'kv_ada_b': nrm((2 * D,), 0.02),
        'kv_norm_g': gain((D,)),
        'sb_w_kv': nrm((D, 2 * SB_WIDTH), D ** -0.5),
        'sb_w_q': nrm((N_B_LAYERS, D, SB_WIDTH), D ** -0.5),
        'sb_logit_bias': SB_BIAS_INIT + nrm((N_B_LAYERS, SB_HEADS), 0.1),
        'sb_w_o': nrm((N_B_LAYERS, SB_WIDTH, D), SB_WIDTH ** -0.5),
    }


def reference(x_prompt, x_sample, c_prompt, c_sample, state_dn_conv, state_dn_S, cache_k, cache_v, page_table,
              ada_w, ada_b, mix_pre_g, mix_post_g, ffn_pre_g, ffn_post_g, ffn_w_in, ffn_w_out,
              dn_w_in, dn_conv_w, dn_a_log, dn_dt_bias, dn_onorm_g, dn_w_out,
              kv_ada_w, kv_ada_b, kv_norm_g, sb_w_kv, sb_w_q, sb_logit_bias, sb_w_o):
    p = {
        'ada_w': ada_w, 'ada_b': ada_b, 'mix_pre_g': mix_pre_g, 'mix_post_g': mix_post_g,
        'ffn_pre_g': ffn_pre_g, 'ffn_post_g': ffn_post_g, 'ffn_w_in': ffn_w_in, 'ffn_w_out': ffn_w_out,
        'dn_w_in': dn_w_in, 'dn_conv_w': dn_conv_w, 'dn_a_log': dn_a_log, 'dn_dt_bias': dn_dt_bias,
        'dn_onorm_g': dn_onorm_g, 'dn_w_out': dn_w_out, 'kv_ada_w': kv_ada_w, 'kv_ada_b': kv_ada_b,
        'kv_norm_g': kv_norm_g, 'sb_w_kv': sb_w_kv, 'sb_w_q': sb_w_q, 'sb_logit_bias': sb_logit_bias,
        'sb_w_o': sb_w_o,
    }
    bp = x_prompt.shape[0]
    zero_buf = jnp.zeros((N_A_LAYERS, bp, DN_CONV - 1, DN_CONV_DIM), jnp.float32)
    zero_s = jnp.zeros((N_A_LAYERS, bp, DN_HEADS, DN_KEY_DIM, DN_VAL_DIM), jnp.float32)
    y_p, buf_p, s_p, k_p, v_p = trunk(x_prompt, c_prompt, zero_buf, zero_s, None, None, p)
    db = x_sample.shape[0]
    k_past = cache_k[page_table].reshape(db, -1, SB_HEADS, HEAD_DIM)
    v_past = cache_v[page_table].reshape(db, -1, SB_HEADS, HEAD_DIM)
    y_s, buf_s, s_s, k_s, v_s = trunk(x_sample, c_sample, state_dn_conv, state_dn_S, k_past, v_past, p)
    return (y_p, y_s, buf_p, s_p, k_p, v_p, buf_s, s_s, k_s, v_s)
```

```python
import functools

import jax
import jax.numpy as jnp
from jax import lax
from jax.experimental import pallas as pl
from jax.experimental.pallas import tpu as pltpu

F32 = jnp.float32
BF16 = jnp.bfloat16

HEAD_DIM = 128
NORM_EPS = 1e-6
L2_EPS = 1e-6
DN_CONV = 4
HIST_ROWS = DN_CONV - 1
ROW_GROUP = 8
DN_CHUNK = 64
INV_BASE = 8
V7X_VMEM_LIMIT = 56 * 1024 * 1024


def _params(sem, vmem=V7X_VMEM_LIMIT):
    return pltpu.CompilerParams(dimension_semantics=sem, vmem_limit_bytes=vmem)


def _dot(a, b):
    return jnp.dot(a, b, preferred_element_type=F32)


def _dot_nt(a, b):
    return lax.dot_general(a, b, (((1,), (1,)), ((), ())), preferred_element_type=F32)


def _dot_tn(a, b):
    return lax.dot_general(a, b, (((0,), (0,)), ((), ())), preferred_element_type=F32)


def _split2(x):
    hi = x.astype(BF16)
    lo = (x - hi.astype(F32)).astype(BF16)
    return hi, lo


def _dot3(a, b):
    a_hi, a_lo = _split2(a)
    b_hi, b_lo = _split2(b)
    return _dot(a_hi, b_hi) + _dot(a_hi, b_lo) + _dot(a_lo, b_hi)


def _sigmoid(x):
    return 1.0 / (1.0 + jnp.exp(-x))


def _silu(x):
    return x * _sigmoid(x)


def _softplus(x):
    return jnp.maximum(x, 0.0) + jnp.log(1.0 + jnp.exp(-jnp.abs(x)))


def _rms_scale(x):
    return lax.rsqrt(jnp.mean(x * x, axis=-1, keepdims=True) + NORM_EPS)


def _const_spec(shape):
    nd = len(shape)
    return pl.BlockSpec(shape, lambda *_: (0,) * nd, pipeline_mode=pl.Buffered(1))


def _mod_spec(arr, tm):
    if arr.shape[1] == 1:
        return pl.BlockSpec((1, 1, arr.shape[2]), lambda b, i: (b, 0, 0))
    return pl.BlockSpec((1, tm, arr.shape[2]), lambda b, i: (0, i, 0))


def _ada_kernel(c_ref, w_ref, b_ref, o_ref):
    c = _silu(c_ref[...])
    c_hi, c_lo = _split2(c)
    w_hi, w_lo = _split2(w_ref[...])
    o_ref[...] = _dot(c_hi, w_hi) + _dot(c_lo, w_hi) + _dot(c_hi, w_lo) + b_ref[...]


def _ada(c, w, b):
    r, d = c.shape
    n = w.shape[1]
    tn = 1024
    return pl.pallas_call(
        _ada_kernel,
        grid=(n // tn,),
        in_specs=[pl.BlockSpec((r, d), lambda j: (0, 0)),
                  pl.BlockSpec((d, tn), lambda j: (0, j)),
                  pl.BlockSpec((1, tn), lambda j: (0, j))],
        out_specs=pl.BlockSpec((r, tn), lambda j: (0, j)),
        out_shape=jax.ShapeDtypeStruct((r, n), F32),
        compiler_params=_params(("arbitrary",)),
        name="ada",
    )(c, w, b.reshape(1, n))


def _dn_in_kernel(x_ref, sc_ref, sh_ref, g_ref, w_ref, wbg_ref, cw_ref, ap_ref, *rest,
                  tm, width, grouped):
    if grouped:
        inj_ref, q_ref, k_ref, v_ref, z_ref, bg_ref, hist_ref, xxo_ref, xx_ref = rest
    else:
        q_ref, k_ref, v_ref, z_ref, bg_ref, hist_ref, xx_ref = rest
    i = pl.program_id(1)
    x = x_ref[...]
    h = (x * _rms_scale(x) * g_ref[...] * (1.0 + sc_ref[0]) + sh_ref[0]).astype(BF16)

    @pl.when(i == 0)
    def _():
        xx_ref[0:ROW_GROUP, :] = jnp.zeros((ROW_GROUP, 3 * width), F32)

    for c in range(3):
        cs = slice(c * width, (c + 1) * width)
        p = _dot(h, w_ref[:, cs])
        if grouped:
            rr = lax.broadcasted_iota(jnp.int32, (tm, 1), 0) % ROW_GROUP
            m = (rr < HIST_ROWS).astype(F32)
            p = inj_ref[:, cs] * m + p * (1.0 - m)
            xxo_ref[:, cs] = p
        xx_ref[ROW_GROUP:ROW_GROUP + tm, cs] = p
    z_ref[...] = _dot(h, w_ref[:, 3 * width:4 * width])

    nh = width // HEAD_DIM
    bgl = _dot(h, wbg_ref[...])
    lane = lax.broadcasted_iota(jnp.int32, bgl.shape, 1)
    beta = _sigmoid(bgl)
    gdec = ap_ref[0:1, :] * _softplus(bgl + ap_ref[1:2, :])
    bg = jnp.where(lane < nh, beta, jnp.where(lane < 2 * nh, gdec, 0.0))
    if grouped:
        rr = lax.broadcasted_iota(jnp.int32, (tm, 1), 0) % ROW_GROUP
        active = jnp.logical_and(rr >= HIST_ROWS, rr < ROW_GROUP - 1).astype(F32)
        bg = bg * active
    bg_ref[...] = bg

    outs = (q_ref, k_ref, v_ref)
    for c in range(3):
        for hd in range(nh):
            cs = slice(c * width + hd * HEAD_DIM, c * width + (hd + 1) * HEAD_DIM)
            y = None
            for t in range(DN_CONV):
                r0 = ROW_GROUP - HIST_ROWS + t
                term = xx_ref[r0:r0 + tm, cs] * cw_ref[t:t + 1, cs]
                y = term if y is None else y + term
            y = _silu(y)
            if c < 2:
                y = y * lax.rsqrt(jnp.sum(y * y, axis=-1, keepdims=True) + L2_EPS)
            if c == 0:
                y = y * (HEAD_DIM ** -0.5)
            outs[c][:, hd * HEAD_DIM:(hd + 1) * HEAD_DIM] = y

    tail = xx_ref[tm:tm + ROW_GROUP, :]
    hist_ref[0] = tail
    xx_ref[0:ROW_GROUP, :] = tail


def _dn_in(x, sc, sh, g, w, wbg, cw, ap, inj, *, nseq, tm):
    t, d = x.shape
    width = w.shape[1] // 4
    nt = t // (nseq * tm)
    grouped = inj is not None
    row = lambda b, i: (b * nt + i, 0)
    in_specs = [pl.BlockSpec((tm, d), row), _mod_spec(sc, tm), _mod_spec(sh, tm),
                _const_spec((1, d)), _const_spec(w.shape), _const_spec(wbg.shape),
                _const_spec(cw.shape), _const_spec(ap.shape)]
    args = [x, sc, sh, g, w, wbg, cw, ap]
    out_shape = [jax.ShapeDtypeStruct((t, width), F32)] * 4 + [
        jax.ShapeDtypeStruct((t, HEAD_DIM), F32),
        jax.ShapeDtypeStruct((nseq, ROW_GROUP, 3 * width), F32)]
    out_specs = [pl.BlockSpec((tm, width), row)] * 4 + [
        pl.BlockSpec((tm, HEAD_DIM), row),
        pl.BlockSpec((1, ROW_GROUP, 3 * width), lambda b, i: (b, 0, 0))]
    if grouped:
        in_specs.append(pl.BlockSpec((tm, 3 * width), row))
        args.append(inj)
        out_shape.append(jax.ShapeDtypeStruct((t, 3 * width), F32))
        out_specs.append(pl.BlockSpec((tm, 3 * width), row))
    return pl.pallas_call(
        functools.partial(_dn_in_kernel, tm=tm, width=width, grouped=grouped),
        grid=(nseq, nt),
        in_specs=in_specs,
        out_specs=out_specs,
        out_shape=out_shape,
        scratch_shapes=[pltpu.VMEM((tm + ROW_GROUP, 3 * width), F32)],
        compiler_params=_params(("arbitrary", "arbitrary")),
        name="dn_in",
    )(*args)


def _delta_kernel(q_ref, k_ref, v_ref, z_ref, bg_ref, s0_ref, og_ref, eb_ref, eg_ref,
                  o_ref, sout_ref, s_scr, bexp_scr, gexp_scr, *, tm, chunk, nh):
    i = pl.program_id(1)
    c_rows = chunk

    @pl.when(i == 0)
    def _():
        s_scr[...] = s0_ref[0]

    bg_hi, bg_lo = _split2(bg_ref[...])
    bexp_scr[...] = _dot(bg_hi, eb_ref[...]) + _dot(bg_lo, eb_ref[...])
    gexp_scr[...] = _dot(bg_hi, eg_ref[...]) + _dot(bg_lo, eg_ref[...])

    ri = lax.broadcasted_iota(jnp.int32, (c_rows, c_rows), 0)
    ci = lax.broadcasted_iota(jnp.int32, (c_rows, c_rows), 1)
    incl = ri >= ci
    strict = ri > ci
    ltri = incl.astype(BF16)
    eye = (ri == ci).astype(F32)
    ones = jnp.ones((c_rows, c_rows), BF16)
    base = min(INV_BASE, c_rows)
    base_mask = (ri // base) == (ci // base)
    nlev = max(base.bit_length() - 2, 0)
    merge_masks = []
    m = base
    while m < c_rows:
        merge_masks.append(jnp.logical_and(ri // (2 * m) == ci // (2 * m), ri // m != ci // m))
        m *= 2

    def chunk_body(c, carry):
        r0 = pl.multiple_of(c * c_rows, c_rows)
        rows = pl.ds(r0, c_rows)
        g_hi, g_lo = _split2(gexp_scr[rows, :])
        gcum = _dot(ltri, g_hi) + _dot(ltri, g_lo)
        for hd in range(nh):
            hs = slice(hd * HEAD_DIM, (hd + 1) * HEAD_DIM)
            q = q_ref[rows, hs]
            k = k_ref[rows, hs]
            v = v_ref[rows, hs]
            b = bexp_scr[rows, hs]
            gc = gcum[:, hs]
            glast = gc[c_rows - 1:c_rows, :]
            eg = jnp.exp(gc)
            kb = k * b
            k_bf = k.astype(BF16)
            gi = gc[:, :c_rows]
            d_hi, d_lo = _split2(gi * eye)
            gj = _dot(ones, d_hi) + _dot(ones, d_lo)
            dec = jnp.where(incl, jnp.exp(jnp.minimum(gi - gj, 0.0)), 0.0)
            a = _dot_nt(kb.astype(BF16), k_bf) * jnp.where(strict, dec, 0.0)
            d = jnp.where(base_mask, a, 0.0)
            p = _dot3(d, d)
            r = -d
            for lev in range(nlev):
                r = r + p + _dot3(r, p)
                if lev + 1 < nlev:
                    p = _dot3(p, p)
            for mask in merge_masks:
                t_bf = (eye + r).astype(BF16)
                x = _dot(t_bf, jnp.where(mask, a, 0.0).astype(BF16))
                r = r - _dot(x.astype(BF16), t_bf)
            rhs = jnp.concatenate([v * b, kb * eg], axis=1)
            uw = rhs + _dot(r.astype(BF16), rhs.astype(BF16))
            u = uw[:, :HEAD_DIM]
            w = uw[:, HEAD_DIM:]
            qk = _dot_nt(q.astype(BF16), k_bf) * dec
            qd = q * eg
            kd = k * jnp.exp(glast - gc)
            s = s_scr[hd]
            wq = _dot(jnp.concatenate([w, qd], axis=0).astype(BF16), s.astype(BF16))
            vnew = u - wq[:c_rows]
            vnew_bf = vnew.astype(BF16)
            o = wq[c_rows:] + _dot(qk.astype(BF16), vnew_bf)
            s_scr[hd] = s * jnp.exp(glast) + _dot_tn(kd.astype(BF16), vnew_bf)
            zz = z_ref[rows, hs]
            on = o * lax.rsqrt(jnp.mean(o * o, axis=-1, keepdims=True) + NORM_EPS) * og_ref[...]
            o_ref[rows, hs] = (on * _silu(zz)).astype(o_ref.dtype)
        return carry

    lax.fori_loop(0, tm // c_rows, chunk_body, 0)

    @pl.when(i == pl.num_programs(1) - 1)
    def _():
        sout_ref[0] = s_scr[...]


def _delta(q, k, v, z, bg, s0, og, *, nseq, tm, chunk):
    t, width = q.shape
    nh = width // HEAD_DIM
    nt = t // (nseq * tm)
    row = lambda b, i: (b * nt + i, 0)
    lane = jnp.arange(width) // HEAD_DIM
    src = jnp.arange(HEAD_DIM)
    eb = (src[:, None] == lane[None, :]).astype(BF16)
    eg = (src[:, None] == lane[None, :] + nh).astype(BF16)
    return pl.pallas_call(
        functools.partial(_delta_kernel, tm=tm, chunk=chunk, nh=nh),
        grid=(nseq, nt),
        in_specs=[pl.BlockSpec((tm, width), row)] * 4 + [
            pl.BlockSpec((tm, HEAD_DIM), row),
            pl.BlockSpec((1, nh, HEAD_DIM, HEAD_DIM), lambda b, i: (b, 0, 0, 0)),
            _const_spec((1, HEAD_DIM)), _const_spec(eb.shape), _const_spec(eg.shape)],
        out_specs=[pl.BlockSpec((tm, width), row),
                   pl.BlockSpec((1, nh, HEAD_DIM, HEAD_DIM), lambda b, i: (b, 0, 0, 0))],
        out_shape=[jax.ShapeDtypeStruct((t, width), BF16 if tm % 16 == 0 else F32),
                   jax.ShapeDtypeStruct((nseq, nh, HEAD_DIM, HEAD_DIM), F32)],
        scratch_shapes=[pltpu.VMEM((nh, HEAD_DIM, HEAD_DIM), F32),
                        pltpu.VMEM((tm, width), F32), pltpu.VMEM((tm, width), F32)],
        compiler_params=_params(("arbitrary", "arbitrary")),
        name="delta",
    )(q, k, v, z, bg, s0, og, eb, eg)


def _mixffn_kernel(mix_ref, x_ref, wo_ref, g1_ref, gt1_ref, g2_ref, sc2_ref, sh2_ref,
                   wi_ref, wout_ref, g3_ref, gt2_ref, y_ref, *, hidden, hchunk):
    a = _dot(mix_ref[...].astype(BF16), wo_ref[...])
    x1 = x_ref[...] + gt1_ref[0] * (a * _rms_scale(a) * g1_ref[...])
    h = (x1 * _rms_scale(x1) * g2_ref[...] * (1.0 + sc2_ref[0]) + sh2_ref[0]).astype(BF16)
    f = None
    for c0 in range(0, hidden, hchunk):
        gate = _dot(h, wi_ref[:, c0:c0 + hchunk])
        up = _dot(h, wi_ref[:, hidden + c0:hidden + c0 + hchunk])
        part = _dot((_silu(gate) * up).astype(BF16), wout_ref[c0:c0 + hchunk, :])
        f = part if f is None else f + part
    y_ref[...] = x1 + gt2_ref[0] * (f * _rms_scale(f) * g3_ref[...])


def _mixffn(mix, x, wo, g1, gt1, g2, sc2, sh2, wi, wout, g3, gt2, *, nseq, tm):
    t, d = x.shape
    nt = t // (nseq * tm)
    hidden = wout.shape[0]
    hchunk = hidden // 4
    row = lambda b, i: (b * nt + i, 0)
    vec = _const_spec((1, d))
    return pl.pallas_call(
        functools.partial(_mixffn_kernel, hidden=hidden, hchunk=hchunk),
        grid=(nseq, nt),
        in_specs=[pl.BlockSpec((tm, mix.shape[1]), row), pl.BlockSpec((tm, d), row),
                  _const_spec(wo.shape), vec, _mod_spec(gt1, tm), vec, _mod_spec(sc2, tm),
                  _mod_spec(sh2, tm), _const_spec(wi.shape), _const_spec(wout.shape), vec,
                  _mod_spec(gt2, tm)],
        out_specs=pl.BlockSpec((tm, d), row),
        out_shape=jax.ShapeDtypeStruct((t, d), F32),
        compiler_params=_params(("arbitrary", "arbitrary")),
        name="mixffn",
    )(mix, x, wo, g1, gt1, g2, sc2, sh2, wi, wout, g3, gt2)


def _attn_in_kernel(x_ref, gq_ref, scq_ref, shq_ref, gk_ref, sck_ref, shk_ref, wq_ref, wkv_ref,
                    q_ref, k_ref, v_ref, kb_ref, vb_ref, *, width):
    x = x_ref[...]
    xn = x * _rms_scale(x)
    hq = (xn * gq_ref[...] * (1.0 + scq_ref[0]) + shq_ref[0]).astype(BF16)
    hk = (xn * gk_ref[...] * (1.0 + sck_ref[0]) + shk_ref[0]).astype(BF16)
    q_ref[...] = (_dot(hq, wq_ref[...]) * (HEAD_DIM ** -0.5)).astype(BF16)
    k = _dot(hk, wkv_ref[:, :width])
    v = _dot(hk, wkv_ref[:, width:])
    k_ref[...] = k
    v_ref[...] = v
    kb_ref[...] = k.astype(BF16)
    vb_ref[...] = v.astype(BF16)


def _attn_in(x, gq, scq, shq, gk, sck, shk, wq, wkv, *, nseq, tm):
    t, d = x.shape
    width = wq.shape[1]
    nt = t // (nseq * tm)
    row = lambda b, i: (b * nt + i, 0)
    vec = _const_spec((1, d))
    blk = pl.BlockSpec((tm, width), row)
    return pl.pallas_call(
        functools.partial(_attn_in_kernel, width=width),
        grid=(nseq, nt),
        in_specs=[pl.BlockSpec((tm, d), row), vec, _mod_spec(scq, tm), _mod_spec(shq, tm),
                  vec, _mod_spec(sck, tm), _mod_spec(shk, tm),
                  _const_spec(wq.shape), _const_spec(wkv.shape)],
        out_specs=[blk] * 5,
        out_shape=[jax.ShapeDtypeStruct((t, width), BF16), jax.ShapeDtypeStruct((t, width), F32),
                   jax.ShapeDtypeStruct((t, width), F32), jax.ShapeDtypeStruct((t, width), BF16),
                   jax.ShapeDtypeStruct((t, width), BF16)],
        compiler_params=_params(("arbitrary", "arbitrary")),
        name="attn_in",
    )(x, gq, scq, shq, gk, sck, shk, wq, wkv)


def _sb_block(q, kb, vb, bias, tri, carry, causal):
    z = _dot_nt(q, kb) + bias
    lf = -_softplus(z)
    if causal:
        ri = lax.broadcasted_iota(jnp.int32, z.shape, 0)
        ci = lax.broadcasted_iota(jnp.int32, z.shape, 1)
        mask = ci < ri
        lf = jnp.where(mask, lf, 0.0)
    after = _dot(lf.astype(BF16), tri) + carry
    att = jnp.exp(z + lf + after)
    if causal:
        att = jnp.where(mask, att, 0.0)
    contrib = _dot(att.astype(BF16), vb)
    return contrib, carry + jnp.sum(lf, axis=-1, keepdims=True)


def _sb_prompt_kernel(bias_ref, q_ref, k_ref, v_ref, o_ref, *, tq):
    hd = pl.program_id(1)
    i = pl.program_id(2)
    q = q_ref[...]
    bias = bias_ref[hd]
    ri = lax.broadcasted_iota(jnp.int32, (tq, tq), 0)
    ci = lax.broadcasted_iota(jnp.int32, (tq, tq), 1)
    tri = (ri > ci).astype(BF16)

    r0 = pl.multiple_of(i * tq, tq)
    acc, carry = _sb_block(q, k_ref[pl.ds(r0, tq), :], v_ref[pl.ds(r0, tq), :], bias, tri,
                           jnp.zeros((tq, 1), F32), True)

    def body(jj, st):
        acc, carry = st
        rj = pl.multiple_of((i - 1 - jj) * tq, tq)
        contrib, carry = _sb_block(q, k_ref[pl.ds(rj, tq), :], v_ref[pl.ds(rj, tq), :], bias, tri,
                                   carry, False)
        return acc + contrib, carry

    acc, carry = lax.fori_loop(0, i, body, (acc, carry))
    o_ref[...] = acc.astype(o_ref.dtype)


def _sb_prompt(q, kb, vb, bias, *, nseq, tq):
    t, width = q.shape
    nh = width // HEAD_DIM
    seq = t // nseq
    nq = seq // tq
    return pl.pallas_call(
        functools.partial(_sb_prompt_kernel, tq=tq),
        grid=(nseq, nh, nq),
        in_specs=[pl.BlockSpec(memory_space=pltpu.SMEM),
                  pl.BlockSpec((tq, HEAD_DIM), lambda b, h, i: (b * nq + i, h)),
                  pl.BlockSpec((seq, HEAD_DIM), lambda b, h, i: (b, h)),
                  pl.BlockSpec((seq, HEAD_DIM), lambda b, h, i: (b, h))],
        out_specs=pl.BlockSpec((tq, HEAD_DIM), lambda b, h, i: (b * nq + i, h)),
        out_shape=jax.ShapeDtypeStruct((t, width), BF16),
        compiler_params=_params(("arbitrary", "arbitrary", "arbitrary")),
        name="sb_prompt",
    )(bias, q, kb, vb)


def _rev_excl_cumsum(x):
    n = x.shape[0]
    group = 16
    if n <= group:
        outs = [None] * n
        acc = jnp.zeros(x.shape[1:], F32)
        for s in range(n - 1, -1, -1):
            outs[s] = acc
            acc = acc + x[s]
        return jnp.stack(outs, axis=0), acc
    ng = n // group
    xg = x.reshape((ng, group) + x.shape[1:])
    outs = [None] * group
    acc = jnp.zeros((ng,) + x.shape[1:], F32)
    for s in range(group - 1, -1, -1):
        outs[s] = acc
        acc = acc + xg[:, s]
    within = jnp.stack(outs, axis=1)
    offs = [None] * ng
    tot = jnp.zeros(x.shape[1:], F32)
    for gidx in range(ng - 1, -1, -1):
        offs[gidx] = tot
        tot = tot + acc[gidx]
    offs = jnp.stack(offs, axis=0)
    return (within + offs[:, None]).reshape(x.shape), tot


def _dec_block(kk, vv, qt, bias, hmask, carry, keymask):
    n = kk.shape[0] // ROW_GROUP
    w = qt.shape[1]
    z = (_dot(kk.astype(BF16), qt) + bias).reshape(n, ROW_GROUP, w)
    lf = -_softplus(z)
    if keymask is not None:
        lf = lf * keymask
    after, total = _rev_excl_cumsum(lf)
    att = jnp.exp(z + lf + after + carry[None]) * hmask[None]
    if keymask is not None:
        att = att * keymask
    att = att.reshape(n * ROW_GROUP, w).astype(BF16)
    return _dot_tn(att, vv.astype(BF16)), carry + total


def _sb_decode_kernel(pt_ref, qt_ref, bias_ref, knew_ref, vnew_ref, *rest, pages_per_step):
    del pt_ref
    pp = pages_per_step
    k_refs = rest[:pp]
    v_refs = rest[pp:2 * pp]
    o_ref, acc_ref, carry_ref = rest[2 * pp:]
    j = pl.program_id(1)
    qt = qt_ref[0]
    bias = bias_ref[...]
    w = qt.shape[1]
    hrow = lax.broadcasted_iota(jnp.int32, (ROW_GROUP, w), 0)
    hcol = lax.broadcasted_iota(jnp.int32, (ROW_GROUP, w), 1) // ROW_GROUP
    hmask = (hrow == hcol).astype(F32)

    @pl.when(j == 0)
    def _():
        shape = (ROW_GROUP, ROW_GROUP, w)
        s_idx = lax.broadcasted_iota(jnp.int32, shape, 0)
        i_idx = lax.broadcasted_iota(jnp.int32, shape, 2) % ROW_GROUP
        real = jnp.logical_and(s_idx >= HIST_ROWS, s_idx < ROW_GROUP - 1)
        keymask = jnp.logical_and(real, s_idx < i_idx).astype(F32)
        contrib, carry = _dec_block(knew_ref[0], vnew_ref[0], qt, bias, hmask,
                                    jnp.zeros((ROW_GROUP, w), F32), keymask)
        acc_ref[...] = contrib
        carry_ref[...] = carry

    acc = acc_ref[...]
    carry = carry_ref[...]
    for r in range(pp):
        contrib, carry = _dec_block(k_refs[r][...], v_refs[r][...], qt, bias, hmask, carry, None)
        acc = acc + contrib
    acc_ref[...] = acc
    carry_ref[...] = carry

    @pl.when(j == pl.num_programs(1) - 1)
    def _():
        o_ref[0] = acc.astype(o_ref.dtype)


def _sb_decode(qt, bias, knew, vnew, cache_k2, cache_v2, page_table, *, n_pool, pages_per_step):
    nb, n_pages = page_table.shape
    pp = pages_per_step
    steps = n_pages // pp
    w = qt.shape[2]
    page_rows = cache_k2.shape[0] // n_pool

    def page_spec(r):
        def imap(b, j, pt):
            return (pt[b * n_pages + (n_pages - 1 - (j * pp + r))], 0)
        return pl.BlockSpec((page_rows, HEAD_DIM), imap)

    seq3 = lambda b, j, pt: (b, 0, 0)
    grid_spec = pltpu.PrefetchScalarGridSpec(
        num_scalar_prefetch=1,
        grid=(nb, steps),
        in_specs=[pl.BlockSpec((1, HEAD_DIM, w), seq3),
                  pl.BlockSpec((1, w), lambda b, j, pt: (0, 0)),
                  pl.BlockSpec((1, ROW_GROUP * ROW_GROUP, HEAD_DIM), seq3),
                  pl.BlockSpec((1, ROW_GROUP * ROW_GROUP, HEAD_DIM), seq3)]
        + [page_spec(r) for r in range(pp)] * 2,
        out_specs=pl.BlockSpec((1, w, HEAD_DIM), seq3),
        scratch_shapes=[pltpu.VMEM((w, HEAD_DIM), F32), pltpu.VMEM((ROW_GROUP, w), F32)],
    )
    return pl.pallas_call(
        functools.partial(_sb_decode_kernel, pages_per_step=pp),
        grid_spec=grid_spec,
        out_shape=jax.ShapeDtypeStruct((nb, w, HEAD_DIM), BF16),
        compiler_params=_params(("arbitrary", "arbitrary")),
        name="sb_decode",
    )(page_table.reshape(-1), qt, bias, knew, vnew,
      *([cache_k2] * pp), *([cache_v2] * pp))


def _mods(m, n, rows):
    parts = jnp.split(m, n, axis=-1)
    if rows is None:
        return [p[:, None, :] for p in parts]
    return [jnp.repeat(p, rows, axis=0)[None] for p in parts]


def _trunk(x2d, m0, m1, mkv, wts, *, nseq, tm, rows_per_mod, dn_nseq, dn_tm, dn_chunk, s0, inj,
           attend):
    sh1, sc1, gt1, sh2, sc2, gt2 = _mods(m0, 6, rows_per_mod)
    dn = _dn_in(x2d, sc1, sh1, wts["mix_pre_g"][0], wts["dn_w_qkvz"], wts["dn_w_bg"],
                wts["dn_conv_w"], wts["dn_ap"], inj, nseq=nseq, tm=tm)
    q, k, v, z, bg, hist = dn[:6]
    xx = dn[6] if inj is not None else None
    o_dn, s_new = _delta(q, k, v, z, bg, s0, wts["dn_onorm_g"], nseq=dn_nseq, tm=dn_tm,
                         chunk=dn_chunk)
    x2 = _mixffn(o_dn, x2d, wts["dn_w_out"], wts["mix_post_g"][0], gt1, wts["ffn_pre_g"][0], sc2, sh2,
                 wts["ffn_w_in"][0], wts["ffn_w_out"][0], wts["ffn_post_g"][0], gt2, nseq=nseq, tm=tm)

    sh1, sc1, gt1, sh2, sc2, gt2 = _mods(m1, 6, rows_per_mod)
    ksh, ksc = _mods(mkv, 2, rows_per_mod)
    qb, k_new, v_new, kb, vb = _attn_in(x2, wts["mix_pre_g"][1], sc1, sh1, wts["kv_norm_g"], ksc, ksh,
                                        wts["sb_w_q"], wts["sb_w_kv"], nseq=nseq, tm=tm)
    o_sb = attend(qb, k_new, v_new, kb, vb)
    y = _mixffn(o_sb, x2, wts["sb_w_o"], wts["mix_post_g"][1], gt1, wts["ffn_pre_g"][1], sc2, sh2,
                wts["ffn_w_in"][1], wts["ffn_w_out"][1], wts["ffn_post_g"][1], gt2, nseq=nseq, tm=tm)
    return y, hist, xx, s_new, k_new, v_new


def kernel(x_prompt, x_sample, c_prompt, c_sample, state_dn_conv, state_dn_S, cache_k, cache_v, page_table, ada_w, ada_b, mix_pre_g, mix_post_g, ffn_pre_g, ffn_post_g, ffn_w_in, ffn_w_out, dn_w_in, dn_conv_w, dn_a_log, dn_dt_bias, dn_onorm_g, dn_w_out, kv_ada_w, kv_ada_b, kv_norm_g, sb_w_kv, sb_w_q, sb_logit_bias, sb_w_o):
    bp, seq, d = x_prompt.shape
    db, dseq, _ = x_sample.shape
    nh = d // HEAD_DIM
    width = nh * HEAD_DIM
    assert ada_w.shape[0] == 2 and dn_w_in.shape[0] == 1 and sb_w_q.shape[0] == 1
    assert dseq == ROW_GROUP - HIST_ROWS - 1 and cache_k.shape[1] == HEAD_DIM and cache_k.shape[2] == nh

    vec = lambda g: g.reshape(g.shape[0], 1, g.shape[1])
    w_in = dn_w_in[0]
    pad_lanes = HEAD_DIM - 2 * nh
    zpad = jnp.zeros((nh,), F32)
    lane_pad = lambda a: jnp.concatenate([zpad, a, jnp.zeros((pad_lanes,), F32)])
    wts = {
        "mix_pre_g": vec(mix_pre_g), "mix_post_g": vec(mix_post_g),
        "ffn_pre_g": vec(ffn_pre_g), "ffn_post_g": vec(ffn_post_g),
        "ffn_w_in": ffn_w_in.astype(BF16), "ffn_w_out": ffn_w_out.astype(BF16),
        "dn_w_qkvz": w_in[:, :4 * width].astype(BF16),
        "dn_w_bg": jnp.pad(w_in[:, 4 * width:], ((0, 0), (0, pad_lanes))).astype(BF16),
        "dn_conv_w": dn_conv_w[0],
        "dn_ap": jnp.stack([lane_pad(-jnp.exp(dn_a_log[0])), lane_pad(dn_dt_bias[0])]),
        "dn_onorm_g": dn_onorm_g,
        "dn_w_out": dn_w_out[0].astype(BF16),
        "kv_norm_g": kv_norm_g.reshape(1, d),
        "sb_w_kv": sb_w_kv.astype(BF16), "sb_w_q": sb_w_q[0].astype(BF16),
        "sb_w_o": sb_w_o[0].astype(BF16),
    }

    c_all = jnp.concatenate([c_prompt, c_sample], axis=0)
    c_all = jnp.pad(c_all, ((0, -(bp + db) % 16), (0, 0)))
    m0 = _ada(c_all, ada_w[0], ada_b[0])
    m1 = _ada(c_all, ada_w[1], ada_b[1])
    mkv = _ada(c_all, kv_ada_w, kv_ada_b)

    tm_p = min(256, seq)
    chunk_p = min(DN_CHUNK, seq)
    dn_tm_p = min(512, seq)
    tq = min(256, seq)
    bias_p = sb_logit_bias[0]

    def attend_prompt(qb, k_new, v_new, kb, vb):
        return _sb_prompt(qb, kb, vb, bias_p, nseq=bp, tq=tq)

    y_p, hist_p, _, s_p, k_p, v_p = _trunk(
        x_prompt.reshape(bp * seq, d), m0[:bp], m1[:bp], mkv[:bp], wts,
        nseq=bp, tm=tm_p, rows_per_mod=None, dn_nseq=bp, dn_tm=dn_tm_p, dn_chunk=chunk_p,
        s0=jnp.zeros((bp, nh, HEAD_DIM, HEAD_DIM), F32), inj=None, attend=attend_prompt)

    rows_s = db * ROW_GROUP
    pad_t = ROW_GROUP - HIST_ROWS - dseq
    xs = jnp.pad(x_sample, ((0, 0), (HIST_ROWS, pad_t), (0, 0))).reshape(rows_s, d)
    inj = jnp.pad(state_dn_conv[0], ((0, 0), (0, ROW_GROUP - HIST_ROWS), (0, 0))).reshape(rows_s, -1)
    n_pool = cache_k.shape[0]
    ck2 = cache_k.reshape(n_pool * HEAD_DIM * nh, HEAD_DIM)
    cv2 = cache_v.reshape(n_pool * HEAD_DIM * nh, HEAD_DIM)
    bias_s = jnp.repeat(sb_logit_bias[0], ROW_GROUP)[None, :]
    n_pages = page_table.shape[1]
    pps = 8 if n_pages % 8 == 0 else 1

    def attend_sample(qb, k_new, v_new, kb, vb):
        qt = qb.reshape(db, ROW_GROUP, nh, HEAD_DIM).transpose(0, 3, 2, 1).reshape(db, HEAD_DIM, nh * ROW_GROUP)
        knew = k_new.reshape(db, ROW_GROUP * nh, HEAD_DIM)
        vnew = v_new.reshape(db, ROW_GROUP * nh, HEAD_DIM)
        o = _sb_decode(qt, bias_s, knew, vnew, ck2, cv2, page_table, n_pool=n_pool, pages_per_step=pps)
        return o.reshape(db, nh, ROW_GROUP, HEAD_DIM).transpose(0, 2, 1, 3).reshape(rows_s, width)

    tm_s = min(256, rows_s)
    y_s, _, xx_s, s_s, k_s, v_s = _trunk(
        xs, m0[bp:bp + db], m1[bp:bp + db], mkv[bp:bp + db], wts,
        nseq=1, tm=tm_s, rows_per_mod=ROW_GROUP, dn_nseq=db, dn_tm=ROW_GROUP, dn_chunk=ROW_GROUP,
        s0=state_dn_S[0], inj=inj, attend=attend_sample)

    real = slice(HIST_ROWS, HIST_ROWS + dseq)
    grp = lambda a: a.reshape((db, ROW_GROUP) + a.shape[1:])
    return (
        y_p.reshape(bp, seq, d),
        grp(y_s)[:, real],
        hist_p[None, :, ROW_GROUP - HIST_ROWS:],
        s_p[None],
        k_p.reshape(bp, seq, nh, HEAD_DIM),
        v_p.reshape(bp, seq, nh, HEAD_DIM),
        grp(xx_s)[None, :, dseq:dseq + HIST_ROWS],
        s_s[None],
        grp(k_s)[:, real].reshape(db, dseq, nh, HEAD_DIM),
        grp(v_s)[:, real].reshape(db, dseq, nh, HEAD_DIM),
    )
```

```python
import functools
import math

import jax
import jax.numpy as jnp
from jax import lax
from jax.experimental import pallas as pl
from jax.experimental.pallas import tpu as pltpu

F32 = jnp.float32
BF16 = jnp.bfloat16

HEAD_DIM = 128
NORM_EPS = 1e-6
L2_EPS = 1e-6
DN_CONV = 4
HIST_ROWS = DN_CONV - 1
ROW_GROUP = 8
DN_CHUNK = 64
DN_UNIT = 256
INV_BASE = 8
LOG2E = math.log2(math.e)
V7X_VMEM_LIMIT = 56 * 1024 * 1024


def _params(sem, vmem=V7X_VMEM_LIMIT):
    return pltpu.CompilerParams(dimension_semantics=sem, vmem_limit_bytes=vmem)


def _dot(a, b):
    return jnp.dot(a, b, preferred_element_type=F32)


def _dot_nt(a, b):
    return lax.dot_general(a, b, (((1,), (1,)), ((), ())), preferred_element_type=F32)


def _dot_tn(a, b):
    return lax.dot_general(a, b, (((0,), (0,)), ((), ())), preferred_element_type=F32)


def _split2(x):
    hi = x.astype(BF16)
    lo = (x - hi.astype(F32)).astype(BF16)
    return hi, lo


def _dot01_l(m01, x):
    hi, lo = _split2(x)
    return _dot(m01, hi) + _dot(m01, lo)


def _dot01_r(x, m01):
    hi, lo = _split2(x)
    return _dot(hi, m01) + _dot(lo, m01)


def _sigmoid(x):
    return 1.0 / (1.0 + jnp.exp(-x))


def _silu(x):
    return x * _sigmoid(x)


def _softplus(x):
    return jnp.maximum(x, 0.0) + jnp.log(1.0 + jnp.exp(-jnp.abs(x)))


def _softplus2(z):
    return jnp.maximum(z, 0.0) + jnp.log(1.0 + jnp.exp2(-jnp.abs(z))) * LOG2E


def _rms_scale(x):
    return lax.rsqrt(jnp.mean(x * x, axis=-1, keepdims=True) + NORM_EPS)


def _const_spec(shape):
    nd = len(shape)
    return pl.BlockSpec(shape, lambda *_: (0,) * nd, pipeline_mode=pl.Buffered(1))


def _mod_spec(arr, tm):
    if arr.shape[1] == 1:
        return pl.BlockSpec((1, 1, arr.shape[2]), lambda b, i: (b, 0, 0))
    return pl.BlockSpec((1, tm, arr.shape[2]), lambda b, i: (0, i, 0))


def _ada_kernel(c_ref, w_ref, b_ref, o_ref):
    c = _silu(c_ref[...])
    c_hi, c_lo = _split2(c)
    w_hi, w_lo = _split2(w_ref[...])
    o_ref[...] = _dot(c_hi, w_hi) + _dot(c_lo, w_hi) + _dot(c_hi, w_lo) + b_ref[...]


def _ada(c, w, b):
    r, d = c.shape
    n = w.shape[1]
    tn = 1024
    return pl.pallas_call(
        _ada_kernel,
        grid=(n // tn,),
        in_specs=[pl.BlockSpec((r, d), lambda j: (0, 0)),
                  pl.BlockSpec((d, tn), lambda j: (0, j)),
                  pl.BlockSpec((1, tn), lambda j: (0, j))],
        out_specs=pl.BlockSpec((r, tn), lambda j: (0, j)),
        out_shape=jax.ShapeDtypeStruct((r, n), F32),
        compiler_params=_params(("arbitrary",)),
        name="ada",
    )(c, w, b.reshape(1, n))


def _dn_in_kernel(x_ref, sc_ref, sh_ref, g_ref, w_ref, wbg_ref, cw_ref, ap_ref, *rest,
                  tm, width, grouped):
    if grouped:
        inj_ref, q_ref, k_ref, v_ref, z_ref, bg_ref, hist_ref, xxo_ref, xx_ref = rest
    else:
        q_ref, k_ref, v_ref, z_ref, bg_ref, hist_ref, xx_ref = rest
    i = pl.program_id(1)
    x = x_ref[...]
    h = (x * _rms_scale(x) * g_ref[...] * (1.0 + sc_ref[0]) + sh_ref[0]).astype(BF16)

    @pl.when(i == 0)
    def _():
        xx_ref[0:ROW_GROUP, :] = jnp.zeros((ROW_GROUP, 3 * width), F32)

    for c in range(3):
        cs = slice(c * width, (c + 1) * width)
        p = _dot(h, w_ref[:, cs])
        if grouped:
            rr = lax.broadcasted_iota(jnp.int32, (tm, 1), 0) % ROW_GROUP
            m = (rr < HIST_ROWS).astype(F32)
            p = inj_ref[:, cs] * m + p * (1.0 - m)
            xxo_ref[:, cs] = p
        xx_ref[ROW_GROUP:ROW_GROUP + tm, cs] = p
    z_ref[...] = _dot(h, w_ref[:, 3 * width:4 * width])

    nh = width // HEAD_DIM
    bgl = _dot(h, wbg_ref[...])
    lane = lax.broadcasted_iota(jnp.int32, bgl.shape, 1)
    beta = _sigmoid(bgl)
    gdec = ap_ref[0:1, :] * _softplus(bgl + ap_ref[1:2, :])
    bg = jnp.where(lane < nh, beta, jnp.where(lane < 2 * nh, gdec, 0.0))
    if grouped:
        rr = lax.broadcasted_iota(jnp.int32, (tm, 1), 0) % ROW_GROUP
        active = jnp.logical_and(rr >= HIST_ROWS, rr < ROW_GROUP - 1).astype(F32)
        bg = bg * active
    bg_ref[...] = bg

    outs = (q_ref, k_ref, v_ref)
    for c in range(3):
        for hd in range(nh):
            cs = slice(c * width + hd * HEAD_DIM, c * width + (hd + 1) * HEAD_DIM)
            y = None
            for t in range(DN_CONV):
                r0 = ROW_GROUP - HIST_ROWS + t
                term = xx_ref[r0:r0 + tm, cs] * cw_ref[t:t + 1, cs]
                y = term if y is None else y + term
            y = _silu(y)
            if c < 2:
                y = y * lax.rsqrt(jnp.sum(y * y, axis=-1, keepdims=True) + L2_EPS)
            if c == 0:
                y = y * (HEAD_DIM ** -0.5)
            outs[c][:, hd * HEAD_DIM:(hd + 1) * HEAD_DIM] = y

    tail = xx_ref[tm:tm + ROW_GROUP, :]
    hist_ref[0] = tail
    xx_ref[0:ROW_GROUP, :] = tail


def _dn_in(x, sc, sh, g, w, wbg, cw, ap, inj, *, nseq, tm):
    t, d = x.shape
    width = w.shape[1] // 4
    nt = t // (nseq * tm)
    grouped = inj is not None
    row = lambda b, i: (b * nt + i, 0)
    in_specs = [pl.BlockSpec((tm, d), row), _mod_spec(sc, tm), _mod_spec(sh, tm),
                _const_spec((1, d)), _const_spec(w.shape), _const_spec(wbg.shape),
                _const_spec(cw.shape), _const_spec(ap.shape)]
    args = [x, sc, sh, g, w, wbg, cw, ap]
    out_shape = [jax.ShapeDtypeStruct((t, width), F32)] * 4 + [
        jax.ShapeDtypeStruct((t, HEAD_DIM), F32),
        jax.ShapeDtypeStruct((nseq, ROW_GROUP, 3 * width), F32)]
    out_specs = [pl.BlockSpec((tm, width), row)] * 4 + [
        pl.BlockSpec((tm, HEAD_DIM), row),
        pl.BlockSpec((1, ROW_GROUP, 3 * width), lambda b, i: (b, 0, 0))]
    if grouped:
        in_specs.append(pl.BlockSpec((tm, 3 * width), row))
        args.append(inj)
        out_shape.append(jax.ShapeDtypeStruct((t, 3 * width), F32))
        out_specs.append(pl.BlockSpec((tm, 3 * width), row))
    return pl.pallas_call(
        functools.partial(_dn_in_kernel, tm=tm, width=width, grouped=grouped),
        grid=(nseq, nt),
        in_specs=in_specs,
        out_specs=out_specs,
        out_shape=out_shape,
        scratch_shapes=[pltpu.VMEM((tm + ROW_GROUP, 3 * width), F32)],
        compiler_params=_params(("arbitrary", "arbitrary")),
        name="dn_in",
    )(*args)


def _delta_group(heads, rows, u, refs, consts, *, chunk, unit, nh, use_t):
    (q_ref, k_ref, v_ref, z_ref, og_ref, o_ref, s_scr, bexp_scr, gcum_scr, grev_scr, gt_scr, mk_ref) = refs
    n_merge, nlev = consts
    hsl = [slice(h * HEAD_DIM, (h + 1) * HEAD_DIM) for h in heads]
    idx = range(len(heads))
    m_strict = mk_ref[0]
    m_incl = mk_ref[1]
    eye = mk_ref[3 + n_merge]

    k = [k_ref[rows, hs] for hs in hsl]
    q = [q_ref[rows, hs] for hs in hsl]
    b = [bexp_scr[rows, hs] for hs in hsl]
    gc = [gcum_scr[rows, hs] for hs in hsl]
    k_bf = [x.astype(BF16) for x in k]
    kb = [k[j] * b[j] for j in idx]
    kk = [_dot_nt(kb[j].astype(BF16), k_bf[j]) for j in idx]
    qkr = [_dot_nt(q[j].astype(BF16), k_bf[j]) for j in idx]
    if use_t:
        gi = [jnp.concatenate([gc[j]] * (unit // HEAD_DIM), axis=1) for j in idx]
        gj = [gt_scr[u, pl.ds(nh + h, 1), :] for h in heads]
    else:
        gi = [gc[j][:, :unit] for j in idx]
        ones = jnp.ones((unit, unit), BF16)
        gj = [_dot01_l(ones, gi[j] * eye) for j in idx]
    dec = [jnp.exp(jnp.minimum(gi[j] - gj[j], 0.0)) for j in idx]
    a = [kk[j] * dec[j] * m_strict for j in idx]
    qk = [(qkr[j] * dec[j] * m_incl).astype(BF16) for j in idx]

    m_base = mk_ref[2]
    d_bf = [(a[j] * m_base).astype(BF16) for j in idx]
    p = [_dot(d_bf[j], d_bf[j]) for j in idx]
    r = [-(a[j] * m_base) for j in idx]
    for lev in range(nlev):
        p_bf = [x.astype(BF16) for x in p]
        r = [r[j] + p[j] + _dot(r[j].astype(BF16), p_bf[j]) for j in idx]
        if lev + 1 < nlev:
            p = [_dot(p_bf[j], p_bf[j]) for j in idx]
    for lvl in range(n_merge):
        m_l = mk_ref[3 + lvl]
        t_bf = [(eye + r[j]).astype(BF16) for j in idx]
        x = [_dot(t_bf[j], (a[j] * m_l).astype(BF16)) for j in idx]
        r = [r[j] - _dot(x[j].astype(BF16), t_bf[j]) for j in idx]

    eg = [jnp.exp(gc[j]) for j in idx]
    rhs = [jnp.concatenate([v_ref[rows, hsl[j]] * b[j], kb[j] * eg[j]], axis=1) for j in idx]
    uw = [rhs[j] + _dot(r[j].astype(BF16), rhs[j].astype(BF16)) for j in idx]
    qd = [q[j] * eg[j] for j in idx]
    kd = [k[j] * jnp.exp(grev_scr[rows, hsl[j]]) for j in idx]

    s = [s_scr[h] for h in heads]
    vn = [[] for _ in idx]
    oq = [[] for _ in idx]
    for c in range(unit // chunk):
        cs = slice(c * chunk, (c + 1) * chunk)
        wq = [_dot(jnp.concatenate([uw[j][cs, HEAD_DIM:], qd[j][cs]], axis=0).astype(BF16),
                   s[j].astype(BF16)) for j in idx]
        for j in idx:
            vn[j].append((uw[j][cs, :HEAD_DIM] - wq[j][:chunk]).astype(BF16))
            oq[j].append(wq[j][chunk:])
        last = (c + 1) * chunk - 1
        s = [s[j] * jnp.exp(gc[j][last:last + 1, :]) + _dot_tn(kd[j][cs].astype(BF16), vn[j][-1])
             for j in idx]
    for j, h in enumerate(heads):
        s_scr[h] = s[j]
    for j in idx:
        o = jnp.concatenate(oq[j], axis=0) + _dot(qk[j], jnp.concatenate(vn[j], axis=0))
        on = o * lax.rsqrt(jnp.mean(o * o, axis=-1, keepdims=True) + NORM_EPS) * og_ref[...]
        o_ref[rows, hsl[j]] = (on * _silu(z_ref[rows, hsl[j]])).astype(o_ref.dtype)


def _delta_kernel(q_ref, k_ref, v_ref, z_ref, bg_ref, s0_ref, og_ref, eb_ref, eg_ref,
                  ltri_ref, urev_ref, ltrit_ref, mk_ref,
                  o_ref, sout_ref, s_scr, bexp_scr, gcum_scr, grev_scr, gt_scr,
                  *, tm, chunk, nh, unit, hgroup):
    i = pl.program_id(1)
    n_units = tm // unit
    base = min(INV_BASE, chunk)
    nlev = max(base.bit_length() - 2, 0)
    n_merge = mk_ref.shape[0] - 4
    use_t = unit % HEAD_DIM == 0

    @pl.when(i == 0)
    def _():
        s_scr[...] = s0_ref[0]

    bg = bg_ref[...]
    bexp_scr[...] = _dot01_r(bg, eb_ref[...])
    gexp = _dot01_r(bg, eg_ref[...])
    if use_t:
        bgt = bg.T
    for u in range(n_units):
        rs = slice(u * unit, (u + 1) * unit)
        g_hi, g_lo = _split2(gexp[rs, :])
        gcum_scr[rs, :] = _dot(ltri_ref[...], g_hi) + _dot(ltri_ref[...], g_lo)
        grev_scr[rs, :] = _dot(urev_ref[...], g_hi) + _dot(urev_ref[...], g_lo)
        if use_t:
            gt_scr[u] = _dot01_r(bgt[:, rs], ltrit_ref[...])

    refs = (q_ref, k_ref, v_ref, z_ref, og_ref, o_ref, s_scr, bexp_scr, gcum_scr, grev_scr, gt_scr, mk_ref)

    def unit_body(u, carry):
        rows = pl.ds(pl.multiple_of(u * unit, unit), unit)
        for g0 in range(0, nh, hgroup):
            _delta_group(list(range(g0, g0 + hgroup)), rows, u, refs, (n_merge, nlev),
                         chunk=chunk, unit=unit, nh=nh, use_t=use_t)
        return carry

    lax.fori_loop(0, n_units, unit_body, 0)

    @pl.when(i == pl.num_programs(1) - 1)
    def _():
        sout_ref[0] = s_scr[...]


def _delta(q, k, v, z, bg, s0, og, *, nseq, tm, chunk):
    t, width = q.shape
    nh = width // HEAD_DIM
    nt = t // (nseq * tm)
    unit = min(DN_UNIT, tm)
    assert tm % unit == 0 and unit % chunk == 0
    row = lambda b, i: (b * nt + i, 0)
    lane = jnp.arange(width) // HEAD_DIM
    src = jnp.arange(HEAD_DIM)
    eb = (src[:, None] == lane[None, :]).astype(BF16)
    eg = (src[:, None] == lane[None, :] + nh).astype(BF16)
    ri = jnp.arange(unit)[:, None]
    ci = jnp.arange(unit)[None, :]
    same = (ri // chunk) == (ci // chunk)
    ltri = jnp.logical_and(same, ri >= ci)
    urev = jnp.logical_and(same, ci > ri)
    base = min(INV_BASE, chunk)
    masks = [jnp.logical_and(same, ri > ci), ltri, (ri // base) == (ci // base)]
    m = base
    while m < chunk:
        masks.append(jnp.logical_and(ri // (2 * m) == ci // (2 * m), ri // m != ci // m))
        m *= 2
    masks.append(ri == ci)
    mk = jnp.stack(masks).astype(F32)
    ltri_b = ltri.astype(BF16)
    return pl.pallas_call(
        functools.partial(_delta_kernel, tm=tm, chunk=chunk, nh=nh, unit=unit, hgroup=min(4, nh)),
        grid=(nseq, nt),
        in_specs=[pl.BlockSpec((tm, width), row)] * 4 + [
            pl.BlockSpec((tm, HEAD_DIM), row),
            pl.BlockSpec((1, nh, HEAD_DIM, HEAD_DIM), lambda b, i: (b, 0, 0, 0)),
            _const_spec((1, HEAD_DIM)), _const_spec(eb.shape), _const_spec(eg.shape),
            _const_spec((unit, unit)), _const_spec((unit, unit)), _const_spec((unit, unit)),
            _const_spec(mk.shape)],
        out_specs=[pl.BlockSpec((tm, width), row),
                   pl.BlockSpec((1, nh, HEAD_DIM, HEAD_DIM), lambda b, i: (b, 0, 0, 0))],
        out_shape=[jax.ShapeDtypeStruct((t, width), BF16 if tm % 16 == 0 else F32),
                   jax.ShapeDtypeStruct((nseq, nh, HEAD_DIM, HEAD_DIM), F32)],
        scratch_shapes=[pltpu.VMEM((nh, HEAD_DIM, HEAD_DIM), F32),
                        pltpu.VMEM((tm, width), F32), pltpu.VMEM((tm, width), F32),
                        pltpu.VMEM((tm, width), F32),
                        pltpu.VMEM((tm // unit, HEAD_DIM, max(unit, HEAD_DIM)), F32)],
        compiler_params=_params(("arbitrary", "arbitrary")),
        name="delta",
    )(q, k, v, z, bg, s0, og, eb, eg, ltri_b, urev.astype(BF16), ltri_b.T, mk)


def _mixffn_kernel(mix_ref, x_ref, wo_ref, g1_ref, gt1_ref, g2_ref, sc2_ref, sh2_ref,
                   wi_ref, wout_ref, g3_ref, gt2_ref, y_ref, *, hidden, hchunk):
    a = _dot(mix_ref[...].astype(BF16), wo_ref[...])
    x1 = x_ref[...] + gt1_ref[0] * (a * _rms_scale(a) * g1_ref[...])
    h = (x1 * _rms_scale(x1) * g2_ref[...] * (1.0 + sc2_ref[0]) + sh2_ref[0]).astype(BF16)
    f = None
    for c0 in range(0, hidden, hchunk):
        gate = _dot(h, wi_ref[:, c0:c0 + hchunk])
        up = _dot(h, wi_ref[:, hidden + c0:hidden + c0 + hchunk])
        part = _dot((_silu(gate) * up).astype(BF16), wout_ref[c0:c0 + hchunk, :])
        f = part if f is None else f + part
    y_ref[...] = x1 + gt2_ref[0] * (f * _rms_scale(f) * g3_ref[...])


def _mixffn(mix, x, wo, g1, gt1, g2, sc2, sh2, wi, wout, g3, gt2, *, nseq, tm):
    t, d = x.shape
    nt = t // (nseq * tm)
    hidden = wout.shape[0]
    hchunk = hidden // 4
    row = lambda b, i: (b * nt + i, 0)
    vec = _const_spec((1, d))
    return pl.pallas_call(
        functools.partial(_mixffn_kernel, hidden=hidden, hchunk=hchunk),
        grid=(nseq, nt),
        in_specs=[pl.BlockSpec((tm, mix.shape[1]), row), pl.BlockSpec((tm, d), row),
                  _const_spec(wo.shape), vec, _mod_spec(gt1, tm), vec, _mod_spec(sc2, tm),
                  _mod_spec(sh2, tm), _const_spec(wi.shape), _const_spec(wout.shape), vec,
                  _mod_spec(gt2, tm)],
        out_specs=pl.BlockSpec((tm, d), row),
        out_shape=jax.ShapeDtypeStruct((t, d), F32),
        compiler_params=_params(("arbitrary", "arbitrary")),
        name="mixffn",
    )(mix, x, wo, g1, gt1, g2, sc2, sh2, wi, wout, g3, gt2)


def _attn_in_kernel(x_ref, gq_ref, scq_ref, shq_ref, gk_ref, sck_ref, shk_ref, wq_ref, wkv_ref,
                    q_ref, k_ref, v_ref, kb_ref, vb_ref, *, width):
    x = x_ref[...]
    xn = x * _rms_scale(x)
    hq = (xn * gq_ref[...] * (1.0 + scq_ref[0]) + shq_ref[0]).astype(BF16)
    hk = (xn * gk_ref[...] * (1.0 + sck_ref[0]) + shk_ref[0]).astype(BF16)
    q_ref[...] = (_dot(hq, wq_ref[...]) * (LOG2E * HEAD_DIM ** -0.5)).astype(BF16)
    k = _dot(hk, wkv_ref[:, :width])
    v = _dot(hk, wkv_ref[:, width:])
    k_ref[...] = k
    v_ref[...] = v
    kb_ref[...] = k.astype(BF16)
    vb_ref[...] = v.astype(BF16)


def _attn_in(x, gq, scq, shq, gk, sck, shk, wq, wkv, *, nseq, tm):
    t, d = x.shape
    width = wq.shape[1]
    nt = t // (nseq * tm)
    row = lambda b, i: (b * nt + i, 0)
    vec = _const_spec((1, d))
    blk = pl.BlockSpec((tm, width), row)
    return pl.pallas_call(
        functools.partial(_attn_in_kernel, width=width),
        grid=(nseq, nt),
        in_specs=[pl.BlockSpec((tm, d), row), vec, _mod_spec(scq, tm), _mod_spec(shq, tm),
                  vec, _mod_spec(sck, tm), _mod_spec(shk, tm),
                  _const_spec(wq.shape), _const_spec(wkv.shape)],
        out_specs=[blk] * 5,
        out_shape=[jax.ShapeDtypeStruct((t, width), BF16), jax.ShapeDtypeStruct((t, width), F32),
                   jax.ShapeDtypeStruct((t, width), F32), jax.ShapeDtypeStruct((t, width), BF16),
                   jax.ShapeDtypeStruct((t, width), BF16)],
        compiler_params=_params(("arbitrary", "arbitrary")),
        name="attn_in",
    )(x, gq, scq, shq, gk, sck, shk, wq, wkv)


def _sb_blocks(qs, ks, vs, biases, tri, carries, causal):
    n = range(len(qs))
    z = [_dot_nt(qs[j], ks[j]) + biases[j] for j in n]
    sp = [_softplus2(z[j]) for j in n]
    if causal:
        ri = lax.broadcasted_iota(jnp.int32, z[0].shape, 0)
        ci = lax.broadcasted_iota(jnp.int32, z[0].shape, 1)
        mask = ci < ri
        sp = [jnp.where(mask, sp[j], 0.0) for j in n]
    after = [_dot(sp[j].astype(BF16), tri) + carries[j] for j in n]
    att = [jnp.exp2(z[j] - sp[j] - after[j]) for j in n]
    if causal:
        att = [jnp.where(mask, att[j], 0.0) for j in n]
    contrib = [_dot(att[j].astype(BF16), vs[j]) for j in n]
    new_carries = [carries[j] + jnp.sum(sp[j], axis=-1, keepdims=True) for j in n]
    return contrib, new_carries


def _sb_prompt_kernel(bias_ref, tri_ref, q_ref, k_ref, v_ref, o_ref, *, tq, hpb):
    hg = pl.program_id(1)
    i = pl.program_id(2)
    hsl = [slice(j * HEAD_DIM, (j + 1) * HEAD_DIM) for j in range(hpb)]
    qs = [q_ref[:, hs] for hs in hsl]
    biases = [bias_ref[hg * hpb + j] * LOG2E for j in range(hpb)]
    tri = tri_ref[...]

    def kv(r0):
        rows = pl.ds(pl.multiple_of(r0, tq), tq)
        return [k_ref[rows, hs] for hs in hsl], [v_ref[rows, hs] for hs in hsl]

    ks, vs = kv(i * tq)
    accs, carries = _sb_blocks(qs, ks, vs, biases, tri, [jnp.zeros((tq, 1), F32)] * hpb, True)

    def body(jj, st):
        accs, carries = st
        ks, vs = kv((i - 1 - jj) * tq)
        contrib, carries = _sb_blocks(qs, ks, vs, biases, tri, list(carries), False)
        return tuple(accs[j] + contrib[j] for j in range(hpb)), tuple(carries)

    accs, carries = lax.fori_loop(0, i, body, (tuple(accs), tuple(carries)))
    for j in range(hpb):
        o_ref[:, hsl[j]] = accs[j].astype(o_ref.dtype)


def _sb_prompt(q, kb, vb, bias, *, nseq, tq, hpb):
    t, width = q.shape
    nh = width // HEAD_DIM
    seq = t // nseq
    nq = seq // tq
    ri = jnp.arange(tq)[:, None]
    ci = jnp.arange(tq)[None, :]
    tri = (ri > ci).astype(BF16)
    wblk = hpb * HEAD_DIM
    return pl.pallas_call(
        functools.partial(_sb_prompt_kernel, tq=tq, hpb=hpb),
        grid=(nseq, nh // hpb, nq),
        in_specs=[pl.BlockSpec(memory_space=pltpu.SMEM),
                  _const_spec((tq, tq)),
                  pl.BlockSpec((tq, wblk), lambda b, h, i: (b * nq + i, h)),
                  pl.BlockSpec((seq, wblk), lambda b, h, i: (b, h)),
                  pl.BlockSpec((seq, wblk), lambda b, h, i: (b, h))],
        out_specs=pl.BlockSpec((tq, wblk), lambda b, h, i: (b * nq + i, h)),
        out_shape=jax.ShapeDtypeStruct((t, width), BF16),
        compiler_params=_params(("arbitrary", "arbitrary", "arbitrary")),
        name="sb_prompt",
    )(bias, tri, q, kb, vb)


def _dec_blocks(kcats, vcats, qbd, bias, carry, keymask):
    nb = range(len(kcats))
    z = [_dot(kcats[p], qbd) + bias for p in nb]
    sp = [_softplus2(z[p]) for p in nb]
    if keymask is not None:
        sp = [jnp.where(keymask, sp[p], 0.0) for p in nb]
    n = z[0].shape[0]
    ri = lax.broadcasted_iota(jnp.int32, (n, n), 0)
    ci = lax.broadcasted_iota(jnp.int32, (n, n), 1)
    mrev = (ci > ri).astype(BF16)
    within = [_dot(mrev, sp[p].astype(BF16)) for p in nb]
    out = None
    for p in nb:
        att = jnp.exp2(z[p] - sp[p] - within[p] - carry)
        if keymask is not None:
            att = jnp.where(keymask, att, 0.0)
        contrib = _dot_tn(att.astype(BF16), vcats[p])
        out = contrib if out is None else out + contrib
        carry = carry + jnp.sum(sp[p], axis=0, keepdims=True)
    return out, carry


def _sb_decode_kernel(pt_ref, qbd_ref, bias_ref, knew_ref, vnew_ref, *rest, pages_per_step, nh):
    del pt_ref
    pp = pages_per_step
    k_refs = rest[:pp]
    v_refs = rest[pp:2 * pp]
    o_ref, acc_ref, carry_ref = rest[2 * pp:]
    j = pl.program_id(1)
    qbd = qbd_ref[0]
    bias = bias_ref[...] * LOG2E
    w = qbd.shape[1]

    @pl.when(j == 0)
    def _():
        s_idx = lax.broadcasted_iota(jnp.int32, (ROW_GROUP, w), 0)
        i_idx = lax.broadcasted_iota(jnp.int32, (ROW_GROUP, w), 1) % ROW_GROUP
        real = jnp.logical_and(s_idx >= HIST_ROWS, s_idx < ROW_GROUP - 1)
        keymask = jnp.logical_and(real, s_idx < i_idx)
        contrib, carry = _dec_blocks([knew_ref[0].astype(BF16)], [vnew_ref[0].astype(BF16)], qbd, bias,
                                     jnp.zeros((1, w), F32), keymask)
        acc_ref[...] = contrib
        carry_ref[...] = carry

    n_tok = k_refs[0].shape[0] // nh

    def cat(ref):
        return jnp.concatenate([ref[pl.ds(h, n_tok, stride=nh), :].astype(BF16) for h in range(nh)], axis=1)

    contrib, carry = _dec_blocks([cat(r) for r in k_refs], [cat(r) for r in v_refs], qbd, bias,
                                 carry_ref[...], None)
    acc = acc_ref[...] + contrib
    acc_ref[...] = acc
    carry_ref[...] = carry

    @pl.when(j == pl.num_programs(1) - 1)
    def _():
        o_ref[0] = jnp.concatenate(
            [acc[h * ROW_GROUP:(h + 1) * ROW_GROUP, h * HEAD_DIM:(h + 1) * HEAD_DIM] for h in range(nh)],
            axis=0).astype(o_ref.dtype)


def _sb_decode(qbd, bias, knew, vnew, cache_k2, cache_v2, page_table, *, n_pool, pages_per_step):
    nb, n_pages = page_table.shape
    pp = pages_per_step
    steps = n_pages // pp
    width, w = qbd.shape[1:]
    nh = width // HEAD_DIM
    page_rows = cache_k2.shape[0] // n_pool

    def page_spec(r):
        def imap(b, j, pt):
            return (pt[b * n_pages + (n_pages - 1 - (j * pp + r))], 0)
        return pl.BlockSpec((page_rows, HEAD_DIM), imap)

    seq3 = lambda b, j, pt: (b, 0, 0)
    grid_spec = pltpu.PrefetchScalarGridSpec(
        num_scalar_prefetch=1,
        grid=(nb, steps),
        in_specs=[pl.BlockSpec((1, width, w), seq3),
                  pl.BlockSpec((1, w), lambda b, j, pt: (0, 0)),
                  pl.BlockSpec((1, ROW_GROUP, width), seq3),
                  pl.BlockSpec((1, ROW_GROUP, width), seq3)]
        + [page_spec(r) for r in range(pp)] * 2,
        out_specs=pl.BlockSpec((1, w, HEAD_DIM), seq3),
        scratch_shapes=[pltpu.VMEM((w, width), F32), pltpu.VMEM((1, w), F32)],
    )
    return pl.pallas_call(
        functools.partial(_sb_decode_kernel, pages_per_step=pp, nh=nh),
        grid_spec=grid_spec,
        out_shape=jax.ShapeDtypeStruct((nb, w, HEAD_DIM), BF16),
        compiler_params=_params(("arbitrary", "arbitrary")),
        name="sb_decode",
    )(page_table.reshape(-1), qbd, bias, knew, vnew,
      *([cache_k2] * pp), *([cache_v2] * pp))


def _mods(m, n, rows):
    parts = jnp.split(m, n, axis=-1)
    if rows is None:
        return [p[:, None, :] for p in parts]
    return [jnp.repeat(p, rows, axis=0)[None] for p in parts]


def _trunk(x2d, m0, m1, mkv, wts, *, nseq, tm, rows_per_mod, dn_nseq, dn_tm, dn_chunk, s0, inj,
           attend):
    sh1, sc1, gt1, sh2, sc2, gt2 = _mods(m0, 6, rows_per_mod)
    dn = _dn_in(x2d, sc1, sh1, wts["mix_pre_g"][0], wts["dn_w_qkvz"], wts["dn_w_bg"],
                wts["dn_conv_w"], wts["dn_ap"], inj, nseq=nseq, tm=tm)
    q, k, v, z, bg, hist = dn[:6]
    xx = dn[6] if inj is not None else None
    o_dn, s_new = _delta(q, k, v, z, bg, s0, wts["dn_onorm_g"], nseq=dn_nseq, tm=dn_tm,
                         chunk=dn_chunk)
    x2 = _mixffn(o_dn, x2d, wts["dn_w_out"], wts["mix_post_g"][0], gt1, wts["ffn_pre_g"][0], sc2, sh2,
                 wts["ffn_w_in"][0], wts["ffn_w_out"][0], wts["ffn_post_g"][0], gt2, nseq=nseq, tm=tm)

    sh1, sc1, gt1, sh2, sc2, gt2 = _mods(m1, 6, rows_per_mod)
    ksh, ksc = _mods(mkv, 2, rows_per_mod)
    qb, k_new, v_new, kb, vb = _attn_in(x2, wts["mix_pre_g"][1], sc1, sh1, wts["kv_norm_g"], ksc, ksh,
                                        wts["sb_w_q"], wts["sb_w_kv"], nseq=nseq, tm=tm)
    o_sb = attend(qb, k_new, v_new, kb, vb)
    y = _mixffn(o_sb, x2, wts["sb_w_o"], wts["mix_post_g"][1], gt1, wts["ffn_pre_g"][1], sc2, sh2,
                wts["ffn_w_in"][1], wts["ffn_w_out"][1], wts["ffn_post_g"][1], gt2, nseq=nseq, tm=tm)
    return y, hist, xx, s_new, k_new, v_new


def kernel(x_prompt, x_sample, c_prompt, c_sample, state_dn_conv, state_dn_S, cache_k, cache_v, page_table, ada_w, ada_b, mix_pre_g, mix_post_g, ffn_pre_g, ffn_post_g, ffn_w_in, ffn_w_out, dn_w_in, dn_conv_w, dn_a_log, dn_dt_bias, dn_onorm_g, dn_w_out, kv_ada_w, kv_ada_b, kv_norm_g, sb_w_kv, sb_w_q, sb_logit_bias, sb_w_o):
    bp, seq, d = x_prompt.shape
    db, dseq, _ = x_sample.shape
    nh = d // HEAD_DIM
    width = nh * HEAD_DIM
    assert ada_w.shape[0] == 2 and dn_w_in.shape[0] == 1 and sb_w_q.shape[0] == 1
    assert dseq == ROW_GROUP - HIST_ROWS - 1 and cache_k.shape[1] == HEAD_DIM and cache_k.shape[2] == nh

    vec = lambda g: g.reshape(g.shape[0], 1, g.shape[1])
    w_in = dn_w_in[0]
    pad_lanes = HEAD_DIM - 2 * nh
    zpad = jnp.zeros((nh,), F32)
    lane_pad = lambda a: jnp.concatenate([zpad, a, jnp.zeros((pad_lanes,), F32)])
    wts = {
        "mix_pre_g": vec(mix_pre_g), "mix_post_g": vec(mix_post_g),
        "ffn_pre_g": vec(ffn_pre_g), "ffn_post_g": vec(ffn_post_g),
        "ffn_w_in": ffn_w_in.astype(BF16), "ffn_w_out": ffn_w_out.astype(BF16),
        "dn_w_qkvz": w_in[:, :4 * width].astype(BF16),
        "dn_w_bg": jnp.pad(w_in[:, 4 * width:], ((0, 0), (0, pad_lanes))).astype(BF16),
        "dn_conv_w": dn_conv_w[0],
        "dn_ap": jnp.stack([lane_pad(-jnp.exp(dn_a_log[0])), lane_pad(dn_dt_bias[0])]),
        "dn_onorm_g": dn_onorm_g,
        "dn_w_out": dn_w_out[0].astype(BF16),
        "kv_norm_g": kv_norm_g.reshape(1, d),
        "sb_w_kv": sb_w_kv.astype(BF16), "sb_w_q": sb_w_q[0].astype(BF16),
        "sb_w_o": sb_w_o[0].astype(BF16),
    }

    c_all = jnp.concatenate([c_prompt, c_sample], axis=0)
    c_all = jnp.pad(c_all, ((0, -(bp + db) % 16), (0, 0)))
    m0 = _ada(c_all, ada_w[0], ada_b[0])
    m1 = _ada(c_all, ada_w[1], ada_b[1])
    mkv = _ada(c_all, kv_ada_w, kv_ada_b)

    tm_p = min(256, seq)
    chunk_p = min(DN_CHUNK, seq)
    dn_tm_p = min(512, seq)
    tq = min(256, seq)
    hpb = min(4, nh)
    bias_p = sb_logit_bias[0]

    def attend_prompt(qb, k_new, v_new, kb, vb):
        return _sb_prompt(qb, kb, vb, bias_p, nseq=bp, tq=tq, hpb=hpb)

    y_p, hist_p, _, s_p, k_p, v_p = _trunk(
        x_prompt.reshape(bp * seq, d), m0[:bp], m1[:bp], mkv[:bp], wts,
        nseq=bp, tm=tm_p, rows_per_mod=None, dn_nseq=bp, dn_tm=dn_tm_p, dn_chunk=chunk_p,
        s0=jnp.zeros((bp, nh, HEAD_DIM, HEAD_DIM), F32), inj=None, attend=attend_prompt)

    rows_s = db * ROW_GROUP
    pad_t = ROW_GROUP - HIST_ROWS - dseq
    xs = jnp.pad(x_sample, ((0, 0), (HIST_ROWS, pad_t), (0, 0))).reshape(rows_s, d)
    inj = jnp.pad(state_dn_conv[0], ((0, 0), (0, ROW_GROUP - HIST_ROWS), (0, 0))).reshape(rows_s, -1)
    n_pool = cache_k.shape[0]
    ck2 = cache_k.reshape(n_pool * HEAD_DIM * nh, HEAD_DIM)
    cv2 = cache_v.reshape(n_pool * HEAD_DIM * nh, HEAD_DIM)
    bias_s = jnp.repeat(sb_logit_bias[0], ROW_GROUP)[None, :]
    n_pages = page_table.shape[1]
    pps = 8 if n_pages % 8 == 0 else 1

    def attend_sample(qb, k_new, v_new, kb, vb):
        qt = qb.reshape(db, ROW_GROUP, nh, HEAD_DIM).transpose(0, 2, 3, 1)
        qbd = (qt[:, :, :, None, :] * jnp.eye(nh, dtype=BF16)[None, :, None, :, None]).reshape(
            db, width, nh * ROW_GROUP)
        knew = k_new.reshape(db, ROW_GROUP, width)
        vnew = v_new.reshape(db, ROW_GROUP, width)
        o = _sb_decode(qbd, bias_s, knew, vnew, ck2, cv2, page_table, n_pool=n_pool, pages_per_step=pps)
        return o.reshape(db, nh, ROW_GROUP, HEAD_DIM).transpose(0, 2, 1, 3).reshape(rows_s, width)

    tm_s = min(256, rows_s)
    y_s, _, xx_s, s_s, k_s, v_s = _trunk(
        xs, m0[bp:bp + db], m1[bp:bp + db], mkv[bp:bp + db], wts,
        nseq=1, tm=tm_s, rows_per_mod=ROW_GROUP, dn_nseq=db, dn_tm=ROW_GROUP, dn_chunk=ROW_GROUP,
        s0=state_dn_S[0], inj=inj, attend=attend_sample)

    real = slice(HIST_ROWS, HIST_ROWS + dseq)
    grp = lambda a: a.reshape((db, ROW_GROUP) + a.shape[1:])
    return (
        y_p.reshape(bp, seq, d),
        grp(y_s)[:, real],
        hist_p[None, :, ROW_GROUP - HIST_ROWS:],
        s_p[None],
        k_p.reshape(bp, seq, nh, HEAD_DIM),
        v_p.reshape(bp, seq, nh, HEAD_DIM),
        grp(xx_s)[None, :, dseq:dseq + HIST_ROWS],
        s_s[None],
        grp(k_s)[:, real].reshape(db, dseq, nh, HEAD_DIM),
        grp(v_s)[:, real].reshape(db, dseq, nh, HEAD_DIM),
    )
```

```python
import functools
import math

import jax
import jax.numpy as jnp
from jax import lax
from jax.experimental import pallas as pl
from jax.experimental.pallas import tpu as pltpu

F32 = jnp.float32
BF16 = jnp.bfloat16

HEAD_DIM = 128
NORM_EPS = 1e-6
L2_EPS = 1e-6
DN_CONV = 4
HIST_ROWS = DN_CONV - 1
ROW_GROUP = 8
DN_CHUNK = 64
DN_UNIT = 256
INV_BASE = 8
LOG2E = math.log2(math.e)
EXP2_CLAMP = 100.0
V7X_VMEM_LIMIT = 56 * 1024 * 1024


def _params(sem, vmem=V7X_VMEM_LIMIT):
    return pltpu.CompilerParams(dimension_semantics=sem, vmem_limit_bytes=vmem)


def _dot(a, b):
    return jnp.dot(a, b, preferred_element_type=F32)


def _dot_nt(a, b):
    return lax.dot_general(a, b, (((1,), (1,)), ((), ())), preferred_element_type=F32)


def _dot_tn(a, b):
    return lax.dot_general(a, b, (((0,), (0,)), ((), ())), preferred_element_type=F32)


def _split2(x):
    hi = x.astype(BF16)
    lo = (x - hi.astype(F32)).astype(BF16)
    return hi, lo


def _dot01_l(m01, x):
    hi, lo = _split2(x)
    return _dot(m01, hi) + _dot(m01, lo)


def _dot01_r(x, m01):
    hi, lo = _split2(x)
    return _dot(hi, m01) + _dot(lo, m01)


def _sigmoid(x):
    return 1.0 / (1.0 + jnp.exp(-x))


def _silu(x):
    return x * _sigmoid(x)


def _softplus(x):
    return jnp.maximum(x, 0.0) + jnp.log(1.0 + jnp.exp(-jnp.abs(x)))


def _softplus2(z):
    e = jnp.exp2(jnp.minimum(z, EXP2_CLAMP))
    return jnp.maximum(jnp.log(1.0 + e) * LOG2E, z)


def _rms_scale(x):
    return lax.rsqrt(jnp.mean(x * x, axis=-1, keepdims=True) + NORM_EPS)


def _const_spec(shape):
    nd = len(shape)
    return pl.BlockSpec(shape, lambda *_: (0,) * nd, pipeline_mode=pl.Buffered(1))


def _mod_spec(arr, tm):
    if arr.shape[1] == 1:
        return pl.BlockSpec((1, 1, arr.shape[2]), lambda b, i: (b, 0, 0))
    return pl.BlockSpec((1, tm, arr.shape[2]), lambda b, i: (0, i, 0))


def _ada_kernel(c_ref, w_ref, b_ref, o_ref):
    c = _silu(c_ref[...])
    c_hi, c_lo = _split2(c)
    w_hi, w_lo = _split2(w_ref[...])
    o_ref[...] = _dot(c_hi, w_hi) + _dot(c_lo, w_hi) + _dot(c_hi, w_lo) + b_ref[...]


def _ada(c, w, b):
    r, d = c.shape
    n = w.shape[1]
    tn = 1024
    return pl.pallas_call(
        _ada_kernel,
        grid=(n // tn,),
        in_specs=[pl.BlockSpec((r, d), lambda j: (0, 0)),
                  pl.BlockSpec((d, tn), lambda j: (0, j)),
                  pl.BlockSpec((1, tn), lambda j: (0, j))],
        out_specs=pl.BlockSpec((r, tn), lambda j: (0, j)),
        out_shape=jax.ShapeDtypeStruct((r, n), F32),
        compiler_params=_params(("arbitrary",)),
        name="ada",
    )(c, w, b.reshape(1, n))


def _dn_in_kernel(x_ref, sc_ref, sh_ref, g_ref, w_ref, wbg_ref, cw_ref, ap_ref, *rest,
                  tm, width, grouped):
    if grouped:
        inj_ref, q_ref, k_ref, v_ref, z_ref, bg_ref, hist_ref, xxo_ref, xx_ref = rest
    else:
        q_ref, k_ref, v_ref, z_ref, bg_ref, hist_ref, xx_ref = rest
    i = pl.program_id(1)
    x = x_ref[...]
    h = (x * _rms_scale(x) * g_ref[...] * (1.0 + sc_ref[0]) + sh_ref[0]).astype(BF16)

    @pl.when(i == 0)
    def _():
        xx_ref[0:ROW_GROUP, :] = jnp.zeros((ROW_GROUP, 3 * width), F32)

    for c in range(3):
        cs = slice(c * width, (c + 1) * width)
        p = _dot(h, w_ref[:, cs])
        if grouped:
            rr = lax.broadcasted_iota(jnp.int32, (tm, 1), 0) % ROW_GROUP
            m = (rr < HIST_ROWS).astype(F32)
            p = inj_ref[:, cs] * m + p * (1.0 - m)
            xxo_ref[:, cs] = p
        xx_ref[ROW_GROUP:ROW_GROUP + tm, cs] = p
    z_ref[...] = _dot(h, w_ref[:, 3 * width:4 * width])

    nh = width // HEAD_DIM
    bgl = _dot(h, wbg_ref[...])
    lane = lax.broadcasted_iota(jnp.int32, bgl.shape, 1)
    beta = _sigmoid(bgl)
    gdec = ap_ref[0:1, :] * _softplus(bgl + ap_ref[1:2, :])
    bg = jnp.where(lane < nh, beta, jnp.where(lane < 2 * nh, gdec, 0.0))
    if grouped:
        rr = lax.broadcasted_iota(jnp.int32, (tm, 1), 0) % ROW_GROUP
        active = jnp.logical_and(rr >= HIST_ROWS, rr < ROW_GROUP - 1).astype(F32)
        bg = bg * active
    bg_ref[...] = bg

    outs = (q_ref, k_ref, v_ref)
    for c in range(3):
        for hd in range(nh):
            cs = slice(c * width + hd * HEAD_DIM, c * width + (hd + 1) * HEAD_DIM)
            full = xx_ref[0:ROW_GROUP + tm, cs]
            y = full[ROW_GROUP:] * cw_ref[DN_CONV - 1:DN_CONV, cs]
            for back in range(1, DN_CONV):
                shifted = pltpu.roll(full, back, axis=0)[ROW_GROUP:]
                y = y + shifted * cw_ref[DN_CONV - 1 - back:DN_CONV - back, cs]
            y = _silu(y)
            if c < 2:
                norm = lax.rsqrt(jnp.sum(y * y, axis=-1, keepdims=True) + L2_EPS)
                y = y * (norm * (HEAD_DIM ** -0.5) if c == 0 else norm)
            outs[c][:, hd * HEAD_DIM:(hd + 1) * HEAD_DIM] = y

    tail = xx_ref[tm:tm + ROW_GROUP, :]
    hist_ref[0] = tail
    xx_ref[0:ROW_GROUP, :] = tail


def _dn_in(x, sc, sh, g, w, wbg, cw, ap, inj, *, nseq, tm):
    t, d = x.shape
    width = w.shape[1] // 4
    nt = t // (nseq * tm)
    grouped = inj is not None
    row = lambda b, i: (b * nt + i, 0)
    in_specs = [pl.BlockSpec((tm, d), row), _mod_spec(sc, tm), _mod_spec(sh, tm),
                _const_spec((1, d)), _const_spec(w.shape), _const_spec(wbg.shape),
                _const_spec(cw.shape), _const_spec(ap.shape)]
    args = [x, sc, sh, g, w, wbg, cw, ap]
    out_shape = [jax.ShapeDtypeStruct((t, width), F32)] * 4 + [
        jax.ShapeDtypeStruct((t, HEAD_DIM), F32),
        jax.ShapeDtypeStruct((nseq, ROW_GROUP, 3 * width), F32)]
    out_specs = [pl.BlockSpec((tm, width), row)] * 4 + [
        pl.BlockSpec((tm, HEAD_DIM), row),
        pl.BlockSpec((1, ROW_GROUP, 3 * width), lambda b, i: (b, 0, 0))]
    if grouped:
        in_specs.append(pl.BlockSpec((tm, 3 * width), row))
        args.append(inj)
        out_shape.append(jax.ShapeDtypeStruct((t, 3 * width), F32))
        out_specs.append(pl.BlockSpec((tm, 3 * width), row))
    return pl.pallas_call(
        functools.partial(_dn_in_kernel, tm=tm, width=width, grouped=grouped),
        grid=(nseq, nt),
        in_specs=in_specs,
        out_specs=out_specs,
        out_shape=out_shape,
        scratch_shapes=[pltpu.VMEM((tm + ROW_GROUP, 3 * width), F32)],
        compiler_params=_params(("arbitrary", "arbitrary")),
        name="dn_in",
    )(*args)


def _delta_group(heads, rows, u, refs, consts, *, chunk, unit, nh, use_t):
    (q_ref, k_ref, v_ref, z_ref, og_ref, o_ref, s_scr, bg_ref, gcum_scr, grev_scr, gt_scr, mk_ref) = refs

    def spread(ref, lane):
        return jnp.broadcast_to(ref[rows, lane:lane + 1], (unit, HEAD_DIM))

    n_merge, nlev = consts
    hsl = [slice(h * HEAD_DIM, (h + 1) * HEAD_DIM) for h in heads]
    idx = range(len(heads))
    m_strict = mk_ref[0]
    m_incl = mk_ref[1]
    eye = mk_ref[3 + n_merge]

    k = [k_ref[rows, hs] for hs in hsl]
    q = [q_ref[rows, hs] for hs in hsl]
    b = [spread(bg_ref, h) for h in heads]
    gc = [spread(gcum_scr, nh + h) for h in heads]
    k_bf = [x.astype(BF16) for x in k]
    kb = [k[j] * b[j] for j in idx]
    kk = [_dot_nt(kb[j].astype(BF16), k_bf[j]) for j in idx]
    qkr = [_dot_nt(q[j].astype(BF16), k_bf[j]) for j in idx]
    if use_t:
        gi = [jnp.concatenate([gc[j]] * (unit // HEAD_DIM), axis=1) for j in idx]
        gj = [gt_scr[u, pl.ds(nh + h, 1), :] for h in heads]
    else:
        gi = [gc[j][:, :unit] for j in idx]
        ones = jnp.ones((unit, unit), BF16)
        gj = [_dot01_l(ones, gi[j] * eye) for j in idx]
    dec = [jnp.exp(jnp.minimum(gi[j] - gj[j], 0.0)) for j in idx]
    a = [kk[j] * dec[j] * m_strict for j in idx]
    qk = [(qkr[j] * dec[j] * m_incl).astype(BF16) for j in idx]

    m_base = mk_ref[2]
    d_bf = [(a[j] * m_base).astype(BF16) for j in idx]
    p = [_dot(d_bf[j], d_bf[j]) for j in idx]
    r = [-(a[j] * m_base) for j in idx]
    for lev in range(nlev):
        p_bf = [x.astype(BF16) for x in p]
        r = [r[j] + p[j] + _dot(r[j].astype(BF16), p_bf[j]) for j in idx]
        if lev + 1 < nlev:
            p = [_dot(p_bf[j], p_bf[j]) for j in idx]
    for lvl in range(n_merge):
        m_l = mk_ref[3 + lvl]
        t_bf = [(eye + r[j]).astype(BF16) for j in idx]
        x = [_dot(t_bf[j], (a[j] * m_l).astype(BF16)) for j in idx]
        r = [r[j] - _dot(x[j].astype(BF16), t_bf[j]) for j in idx]

    eg = [jnp.exp(gc[j]) for j in idx]
    rhs = [jnp.concatenate([v_ref[rows, hsl[j]] * b[j], kb[j] * eg[j]], axis=1) for j in idx]
    uw = [rhs[j] + _dot(r[j].astype(BF16), rhs[j].astype(BF16)) for j in idx]
    qd = [q[j] * eg[j] for j in idx]
    kd = [k[j] * jnp.exp(spread(grev_scr, nh + h)) for j, h in enumerate(heads)]

    s = [s_scr[h] for h in heads]
    vn = [[] for _ in idx]
    oq = [[] for _ in idx]
    for c in range(unit // chunk):
        cs = slice(c * chunk, (c + 1) * chunk)
        wq = [_dot(jnp.concatenate([uw[j][cs, HEAD_DIM:], qd[j][cs]], axis=0).astype(BF16),
                   s[j].astype(BF16)) for j in idx]
        for j in idx:
            vn[j].append((uw[j][cs, :HEAD_DIM] - wq[j][:chunk]).astype(BF16))
            oq[j].append(wq[j][chunk:])
        last = (c + 1) * chunk - 1
        s = [s[j] * jnp.exp(gc[j][last:last + 1, :]) + _dot_tn(kd[j][cs].astype(BF16), vn[j][-1])
             for j in idx]
    for j, h in enumerate(heads):
        s_scr[h] = s[j]
    for j in idx:
        o = jnp.concatenate(oq[j], axis=0) + _dot(qk[j], jnp.concatenate(vn[j], axis=0))
        on = o * lax.rsqrt(jnp.mean(o * o, axis=-1, keepdims=True) + NORM_EPS) * og_ref[...]
        o_ref[rows, hsl[j]] = (on * _silu(z_ref[rows, hsl[j]])).astype(o_ref.dtype)


def _delta_kernel(q_ref, k_ref, v_ref, z_ref, bg_ref, s0_ref, og_ref,
                  ltri_ref, urev_ref, ltrit_ref, mk_ref,
                  o_ref, sout_ref, s_scr, gcum_scr, grev_scr, gt_scr,
                  *, tm, chunk, nh, unit, hgroup):
    i = pl.program_id(1)
    n_units = tm // unit
    base = min(INV_BASE, chunk)
    nlev = max(base.bit_length() - 2, 0)
    n_merge = mk_ref.shape[0] - 4
    use_t = unit % HEAD_DIM == 0

    @pl.when(i == 0)
    def _():
        s_scr[...] = s0_ref[0]

    bg = bg_ref[...]
    if use_t:
        bgt = bg.T
    for u in range(n_units):
        rs = slice(u * unit, (u + 1) * unit)
        g_hi, g_lo = _split2(bg[rs, :])
        gcum_scr[rs, :] = _dot(ltri_ref[...], g_hi) + _dot(ltri_ref[...], g_lo)
        grev_scr[rs, :] = _dot(urev_ref[...], g_hi) + _dot(urev_ref[...], g_lo)
        if use_t:
            gt_scr[u] = _dot01_r(bgt[:, rs], ltrit_ref[...])

    refs = (q_ref, k_ref, v_ref, z_ref, og_ref, o_ref, s_scr, bg_ref, gcum_scr, grev_scr, gt_scr, mk_ref)

    def unit_body(u, carry):
        rows = pl.ds(pl.multiple_of(u * unit, unit), unit)
        for g0 in range(0, nh, hgroup):
            _delta_group(list(range(g0, g0 + hgroup)), rows, u, refs, (n_merge, nlev),
                         chunk=chunk, unit=unit, nh=nh, use_t=use_t)
        return carry

    lax.fori_loop(0, n_units, unit_body, 0)

    @pl.when(i == pl.num_programs(1) - 1)
    def _():
        sout_ref[0] = s_scr[...]


def _delta(q, k, v, z, bg, s0, og, *, nseq, tm, chunk):
    t, width = q.shape
    nh = width // HEAD_DIM
    nt = t // (nseq * tm)
    unit = min(DN_UNIT, tm)
    assert tm % unit == 0 and unit % chunk == 0
    row = lambda b, i: (b * nt + i, 0)
    ri = jnp.arange(unit)[:, None]
    ci = jnp.arange(unit)[None, :]
    same = (ri // chunk) == (ci // chunk)
    ltri = jnp.logical_and(same, ri >= ci)
    urev = jnp.logical_and(same, ci > ri)
    base = min(INV_BASE, chunk)
    masks = [jnp.logical_and(same, ri > ci), ltri, (ri // base) == (ci // base)]
    m = base
    while m < chunk:
        masks.append(jnp.logical_and(ri // (2 * m) == ci // (2 * m), ri // m != ci // m))
        m *= 2
    masks.append(ri == ci)
    mk = jnp.stack(masks).astype(F32)
    ltri_b = ltri.astype(BF16)
    return pl.pallas_call(
        functools.partial(_delta_kernel, tm=tm, chunk=chunk, nh=nh, unit=unit, hgroup=min(4, nh)),
        grid=(nseq, nt),
        in_specs=[pl.BlockSpec((tm, width), row)] * 4 + [
            pl.BlockSpec((tm, HEAD_DIM), row),
            pl.BlockSpec((1, nh, HEAD_DIM, HEAD_DIM), lambda b, i: (b, 0, 0, 0)),
            _const_spec((1, HEAD_DIM)),
            _const_spec((unit, unit)), _const_spec((unit, unit)), _const_spec((unit, unit)),
            _const_spec(mk.shape)],
        out_specs=[pl.BlockSpec((tm, width), row),
                   pl.BlockSpec((1, nh, HEAD_DIM, HEAD_DIM), lambda b, i: (b, 0, 0, 0))],
        out_shape=[jax.ShapeDtypeStruct((t, width), BF16 if tm % 16 == 0 else F32),
                   jax.ShapeDtypeStruct((nseq, nh, HEAD_DIM, HEAD_DIM), F32)],
        scratch_shapes=[pltpu.VMEM((nh, HEAD_DIM, HEAD_DIM), F32),
                        pltpu.VMEM((tm, HEAD_DIM), F32), pltpu.VMEM((tm, HEAD_DIM), F32),
                        pltpu.VMEM((tm // unit, HEAD_DIM, max(unit, HEAD_DIM)), F32)],
        compiler_params=_params(("arbitrary", "arbitrary")),
        name="delta",
    )(q, k, v, z, bg, s0, og, ltri_b, urev.astype(BF16), ltri_b.T, mk)


def _mixffn_kernel(mix_ref, x_ref, wo_ref, g1_ref, gt1_ref, g2_ref, sc2_ref, sh2_ref,
                   wi_ref, wout_ref, g3_ref, gt2_ref, y_ref, *, hidden, hchunk):
    a = _dot(mix_ref[...].astype(BF16), wo_ref[...])
    x1 = x_ref[...] + gt1_ref[0] * (a * _rms_scale(a) * g1_ref[...])
    h = (x1 * _rms_scale(x1) * g2_ref[...] * (1.0 + sc2_ref[0]) + sh2_ref[0]).astype(BF16)
    f = None
    for c0 in range(0, hidden, hchunk):
        gate = _dot(h, wi_ref[:, c0:c0 + hchunk])
        up = _dot(h, wi_ref[:, hidden + c0:hidden + c0 + hchunk])
        part = _dot((_silu(gate) * up).astype(BF16), wout_ref[c0:c0 + hchunk, :])
        f = part if f is None else f + part
    y_ref[...] = x1 + gt2_ref[0] * (f * _rms_scale(f) * g3_ref[...])


def _mixffn(mix, x, wo, g1, gt1, g2, sc2, sh2, wi, wout, g3, gt2, *, nseq, tm):
    t, d = x.shape
    nt = t // (nseq * tm)
    hidden = wout.shape[0]
    hchunk = hidden // 4
    row = lambda b, i: (b * nt + i, 0)
    vec = _const_spec((1, d))
    return pl.pallas_call(
        functools.partial(_mixffn_kernel, hidden=hidden, hchunk=hchunk),
        grid=(nseq, nt),
        in_specs=[pl.BlockSpec((tm, mix.shape[1]), row), pl.BlockSpec((tm, d), row),
                  _const_spec(wo.shape), vec, _mod_spec(gt1, tm), vec, _mod_spec(sc2, tm),
                  _mod_spec(sh2, tm), _const_spec(wi.shape), _const_spec(wout.shape), vec,
                  _mod_spec(gt2, tm)],
        out_specs=pl.BlockSpec((tm, d), row),
        out_shape=jax.ShapeDtypeStruct((t, d), F32),
        compiler_params=_params(("arbitrary", "arbitrary")),
        name="mixffn",
    )(mix, x, wo, g1, gt1, g2, sc2, sh2, wi, wout, g3, gt2)


def _attn_in_kernel(x_ref, gq_ref, scq_ref, shq_ref, gk_ref, sck_ref, shk_ref, wq_ref, wkv_ref,
                    q_ref, k_ref, v_ref, kb_ref, vb_ref, *, width):
    x = x_ref[...]
    xn = x * _rms_scale(x)
    hq = (xn * gq_ref[...] * (1.0 + scq_ref[0]) + shq_ref[0]).astype(BF16)
    hk = (xn * gk_ref[...] * (1.0 + sck_ref[0]) + shk_ref[0]).astype(BF16)
    q_ref[...] = (_dot(hq, wq_ref[...]) * (LOG2E * HEAD_DIM ** -0.5)).astype(BF16)
    k = _dot(hk, wkv_ref[:, :width])
    v = _dot(hk, wkv_ref[:, width:])
    k_ref[...] = k
    v_ref[...] = v
    kb_ref[...] = k.astype(BF16)
    vb_ref[...] = v.astype(BF16)


def _attn_in(x, gq, scq, shq, gk, sck, shk, wq, wkv, *, nseq, tm):
    t, d = x.shape
    width = wq.shape[1]
    nt = t // (nseq * tm)
    row = lambda b, i: (b * nt + i, 0)
    vec = _const_spec((1, d))
    blk = pl.BlockSpec((tm, width), row)
    return pl.pallas_call(
        functools.partial(_attn_in_kernel, width=width),
        grid=(nseq, nt),
        in_specs=[pl.BlockSpec((tm, d), row), vec, _mod_spec(scq, tm), _mod_spec(shq, tm),
                  vec, _mod_spec(sck, tm), _mod_spec(shk, tm),
                  _const_spec(wq.shape), _const_spec(wkv.shape)],
        out_specs=[blk] * 5,
        out_shape=[jax.ShapeDtypeStruct((t, width), BF16), jax.ShapeDtypeStruct((t, width), F32),
                   jax.ShapeDtypeStruct((t, width), F32), jax.ShapeDtypeStruct((t, width), BF16),
                   jax.ShapeDtypeStruct((t, width), BF16)],
        compiler_params=_params(("arbitrary", "arbitrary")),
        name="attn_in",
    )(x, gq, scq, shq, gk, sck, shk, wq, wkv)


def _causal_mask(shape):
    ri = lax.broadcasted_iota(jnp.int32, shape, 0)
    ci = lax.broadcasted_iota(jnp.int32, shape, 1)
    return ci < ri


def _sb_blocks(qs, ks, vs, biases, tri, carries, causal):
    n = range(len(qs))
    sp_bf, lsig, new_carries = [], [], []
    for j in n:
        z = _dot_nt(qs[j], ks[j]) + biases[j]
        sp = _softplus2(z)
        lsig.append(z - sp - carries[j])
        if causal:
            mask = _causal_mask(z.shape)
            sp = jnp.where(mask, sp, 0.0)
        new_carries.append(carries[j] + jnp.sum(sp, axis=-1, keepdims=True))
        sp_bf.append(sp.astype(BF16))
    att = [jnp.exp2(lsig[j] - _dot(sp_bf[j], tri)) for j in n]
    if causal:
        att = [jnp.where(mask, att[j], 0.0) for j in n]
    contrib = [_dot(att[j].astype(BF16), vs[j]) for j in n]
    return contrib, new_carries


def _sb_prompt_kernel(bias_ref, tri_ref, q_ref, k_ref, v_ref, o_ref, *, tq, hpb):
    hg = pl.program_id(1)
    i = pl.program_id(2)
    hsl = [slice(j * HEAD_DIM, (j + 1) * HEAD_DIM) for j in range(hpb)]
    qs = [q_ref[:, hs] for hs in hsl]
    biases = [bias_ref[hg * hpb + j] * LOG2E for j in range(hpb)]
    tri = tri_ref[...]

    def kv(r0):
        rows = pl.ds(pl.multiple_of(r0, tq), tq)
        return [k_ref[rows, hs] for hs in hsl], [v_ref[rows, hs] for hs in hsl]

    ks, vs = kv(i * tq)
    accs, carries = _sb_blocks(qs, ks, vs, biases, tri, [jnp.zeros((tq, 1), F32)] * hpb, True)

    def body(jj, st):
        accs, carries = st
        ks, vs = kv((i - 1 - jj) * tq)
        contrib, carries = _sb_blocks(qs, ks, vs, biases, tri, list(carries), False)
        return tuple(accs[j] + contrib[j] for j in range(hpb)), tuple(carries)

    accs, carries = lax.fori_loop(0, i, body, (tuple(accs), tuple(carries)))
    for j in range(hpb):
        o_ref[:, hsl[j]] = accs[j].astype(o_ref.dtype)


def _sb_prompt(q, kb, vb, bias, *, nseq, tq, hpb):
    t, width = q.shape
    nh = width // HEAD_DIM
    seq = t // nseq
    nq = seq // tq
    ri = jnp.arange(tq)[:, None]
    ci = jnp.arange(tq)[None, :]
    tri = (ri > ci).astype(BF16)
    wblk = hpb * HEAD_DIM
    return pl.pallas_call(
        functools.partial(_sb_prompt_kernel, tq=tq, hpb=hpb),
        grid=(nseq, nh // hpb, nq),
        in_specs=[pl.BlockSpec(memory_space=pltpu.SMEM),
                  _const_spec((tq, tq)),
                  pl.BlockSpec((tq, wblk), lambda b, h, i: (b * nq + i, h)),
                  pl.BlockSpec((seq, wblk), lambda b, h, i: (b, h)),
                  pl.BlockSpec((seq, wblk), lambda b, h, i: (b, h))],
        out_specs=pl.BlockSpec((tq, wblk), lambda b, h, i: (b * nq + i, h)),
        out_shape=jax.ShapeDtypeStruct((t, width), BF16),
        compiler_params=_params(("arbitrary", "arbitrary", "arbitrary")),
        name="sb_prompt",
    )(bias, tri, q, kb, vb)


def _dec_blocks(kcats, vcats, qbd, bias, carry, keymask):
    nb = range(len(kcats))
    z = [_dot_nt(qbd, kcats[p]) + bias for p in nb]
    sp = [_softplus2(z[p]) for p in nb]
    if keymask is not None:
        sp = [jnp.where(keymask, sp[p], 0.0) for p in nb]
    n = z[0].shape[1]
    ri = lax.broadcasted_iota(jnp.int32, (n, n), 0)
    ci = lax.broadcasted_iota(jnp.int32, (n, n), 1)
    mrev = (ri > ci).astype(BF16)
    within = [_dot(sp[p].astype(BF16), mrev) for p in nb]
    out = None
    for p in nb:
        att = jnp.exp2(z[p] - sp[p] - within[p] - carry)
        if keymask is not None:
            att = jnp.where(keymask, att, 0.0)
        contrib = _dot(att.astype(BF16), vcats[p])
        out = contrib if out is None else out + contrib
        carry = carry + jnp.sum(sp[p], axis=1, keepdims=True)
    return out, carry


def _sb_decode_kernel(pt_ref, qbd_ref, bias_ref, knew_ref, vnew_ref, *rest, pages_per_step, nh):
    del pt_ref
    pp = pages_per_step
    k_refs = rest[:pp]
    v_refs = rest[pp:2 * pp]
    o_ref, acc_ref, carry_ref = rest[2 * pp:]
    j = pl.program_id(1)
    qbd = qbd_ref[0]
    bias = bias_ref[...] * LOG2E
    w = qbd.shape[0]

    @pl.when(j == 0)
    def _():
        s_idx = lax.broadcasted_iota(jnp.int32, (w, ROW_GROUP), 1)
        i_idx = lax.broadcasted_iota(jnp.int32, (w, ROW_GROUP), 0) % ROW_GROUP
        real = jnp.logical_and(s_idx >= HIST_ROWS, s_idx < ROW_GROUP - 1)
        keymask = jnp.logical_and(real, s_idx < i_idx)
        contrib, carry = _dec_blocks([knew_ref[0].astype(BF16)], [vnew_ref[0].astype(BF16)], qbd, bias,
                                     jnp.zeros((w, 1), F32), keymask)
        acc_ref[...] = contrib
        carry_ref[...] = jnp.broadcast_to(carry, carry_ref.shape)

    n_tok = k_refs[0].shape[0] // nh

    def cat(ref):
        return jnp.concatenate([ref[pl.ds(h, n_tok, stride=nh), :].astype(BF16) for h in range(nh)], axis=1)

    contrib, carry = _dec_blocks([cat(r) for r in k_refs], [cat(r) for r in v_refs], qbd, bias,
                                 carry_ref[:, 0:1], None)
    acc = acc_ref[...] + contrib
    acc_ref[...] = acc
    carry_ref[...] = jnp.broadcast_to(carry, carry_ref.shape)

    @pl.when(j == pl.num_programs(1) - 1)
    def _():
        o_ref[0] = jnp.concatenate(
            [acc[h * ROW_GROUP:(h + 1) * ROW_GROUP, h * HEAD_DIM:(h + 1) * HEAD_DIM] for h in range(nh)],
            axis=0).astype(o_ref.dtype)


def _sb_decode(qbd, bias, knew, vnew, cache_k2, cache_v2, page_table, *, n_pool, pages_per_step):
    nb, n_pages = page_table.shape
    pp = pages_per_step
    steps = n_pages // pp
    w, width = qbd.shape[1:]
    nh = width // HEAD_DIM
    page_rows = cache_k2.shape[0] // n_pool

    def page_spec(r):
        def imap(b, j, pt):
            return (pt[b * n_pages + (n_pages - 1 - (j * pp + r))], 0)
        return pl.BlockSpec((page_rows, HEAD_DIM), imap)

    seq3 = lambda b, j, pt: (b, 0, 0)
    grid_spec = pltpu.PrefetchScalarGridSpec(
        num_scalar_prefetch=1,
        grid=(nb, steps),
        in_specs=[pl.BlockSpec((1, w, width), seq3),
                  pl.BlockSpec((w, 1), lambda b, j, pt: (0, 0)),
                  pl.BlockSpec((1, ROW_GROUP, width), seq3),
                  pl.BlockSpec((1, ROW_GROUP, width), seq3)]
        + [page_spec(r) for r in range(pp)] * 2,
        out_specs=pl.BlockSpec((1, w, HEAD_DIM), seq3),
        scratch_shapes=[pltpu.VMEM((w, width), F32), pltpu.VMEM((w, HEAD_DIM), F32)],
    )
    return pl.pallas_call(
        functools.partial(_sb_decode_kernel, pages_per_step=pp, nh=nh),
        grid_spec=grid_spec,
        out_shape=jax.ShapeDtypeStruct((nb, w, HEAD_DIM), BF16),
        compiler_params=_params(("arbitrary", "arbitrary")),
        name="sb_decode",
    )(page_table.reshape(-1), qbd, bias, knew, vnew,
      *([cache_k2] * pp), *([cache_v2] * pp))


def _mods(m, n, rows):
    parts = jnp.split(m, n, axis=-1)
    if rows is None:
        return [p[:, None, :] for p in parts]
    return [jnp.repeat(p, rows, axis=0)[None] for p in parts]


def _trunk(x2d, m0, m1, mkv, wts, *, nseq, tm, rows_per_mod, dn_nseq, dn_tm, dn_chunk, s0, inj,
           attend):
    sh1, sc1, gt1, sh2, sc2, gt2 = _mods(m0, 6, rows_per_mod)
    dn = _dn_in(x2d, sc1, sh1, wts["mix_pre_g"][0], wts["dn_w_qkvz"], wts["dn_w_bg"],
                wts["dn_conv_w"], wts["dn_ap"], inj, nseq=nseq, tm=tm)
    q, k, v, z, bg, hist = dn[:6]
    xx = dn[6] if inj is not None else None
    o_dn, s_new = _delta(q, k, v, z, bg, s0, wts["dn_onorm_g"], nseq=dn_nseq, tm=dn_tm,
                         chunk=dn_chunk)
    x2 = _mixffn(o_dn, x2d, wts["dn_w_out"], wts["mix_post_g"][0], gt1, wts["ffn_pre_g"][0], sc2, sh2,
                 wts["ffn_w_in"][0], wts["ffn_w_out"][0], wts["ffn_post_g"][0], gt2, nseq=nseq, tm=tm)

    sh1, sc1, gt1, sh2, sc2, gt2 = _mods(m1, 6, rows_per_mod)
    ksh, ksc = _mods(mkv, 2, rows_per_mod)
    qb, k_new, v_new, kb, vb = _attn_in(x2, wts["mix_pre_g"][1], sc1, sh1, wts["kv_norm_g"], ksc, ksh,
                                        wts["sb_w_q"], wts["sb_w_kv"], nseq=nseq, tm=tm)
    o_sb = attend(qb, k_new, v_new, kb, vb)
    y = _mixffn(o_sb, x2, wts["sb_w_o"], wts["mix_post_g"][1], gt1, wts["ffn_pre_g"][1], sc2, sh2,
                wts["ffn_w_in"][1], wts["ffn_w_out"][1], wts["ffn_post_g"][1], gt2, nseq=nseq, tm=tm)
    return y, hist, xx, s_new, k_new, v_new


def kernel(x_prompt, x_sample, c_prompt, c_sample, state_dn_conv, state_dn_S, cache_k, cache_v, page_table, ada_w, ada_b, mix_pre_g, mix_post_g, ffn_pre_g, ffn_post_g, ffn_w_in, ffn_w_out, dn_w_in, dn_conv_w, dn_a_log, dn_dt_bias, dn_onorm_g, dn_w_out, kv_ada_w, kv_ada_b, kv_norm_g, sb_w_kv, sb_w_q, sb_logit_bias, sb_w_o):
    bp, seq, d = x_prompt.shape
    db, dseq, _ = x_sample.shape
    nh = d // HEAD_DIM
    width = nh * HEAD_DIM
    assert ada_w.shape[0] == 2 and dn_w_in.shape[0] == 1 and sb_w_q.shape[0] == 1
    assert dseq == ROW_GROUP - HIST_ROWS - 1 and cache_k.shape[1] == HEAD_DIM and cache_k.shape[2] == nh

    vec = lambda g: g.reshape(g.shape[0], 1, g.shape[1])
    w_in = dn_w_in[0]
    pad_lanes = HEAD_DIM - 2 * nh
    zpad = jnp.zeros((nh,), F32)
    lane_pad = lambda a: jnp.concatenate([zpad, a, jnp.zeros((pad_lanes,), F32)])
    wts = {
        "mix_pre_g": vec(mix_pre_g), "mix_post_g": vec(mix_post_g),
        "ffn_pre_g": vec(ffn_pre_g), "ffn_post_g": vec(ffn_post_g),
        "ffn_w_in": ffn_w_in.astype(BF16), "ffn_w_out": ffn_w_out.astype(BF16),
        "dn_w_qkvz": w_in[:, :4 * width].astype(BF16),
        "dn_w_bg": jnp.pad(w_in[:, 4 * width:], ((0, 0), (0, pad_lanes))).astype(BF16),
        "dn_conv_w": dn_conv_w[0],
        "dn_ap": jnp.stack([lane_pad(-jnp.exp(dn_a_log[0])), lane_pad(dn_dt_bias[0])]),
        "dn_onorm_g": dn_onorm_g,
        "dn_w_out": dn_w_out[0].astype(BF16),
        "kv_norm_g": kv_norm_g.reshape(1, d),
        "sb_w_kv": sb_w_kv.astype(BF16), "sb_w_q": sb_w_q[0].astype(BF16),
        "sb_w_o": sb_w_o[0].astype(BF16),
    }

    c_all = jnp.concatenate([c_prompt, c_sample], axis=0)
    c_all = jnp.pad(c_all, ((0, -(bp + db) % 16), (0, 0)))
    m0 = _ada(c_all, ada_w[0], ada_b[0])
    m1 = _ada(c_all, ada_w[1], ada_b[1])
    mkv = _ada(c_all, kv_ada_w, kv_ada_b)

    tm_p = min(256, seq)
    chunk_p = min(DN_CHUNK, seq)
    dn_tm_p = min(512, seq)
    tq = min(256, seq)
    hpb = min(4, nh)
    bias_p = sb_logit_bias[0]

    def attend_prompt(qb, k_new, v_new, kb, vb):
        return _sb_prompt(qb, kb, vb, bias_p, nseq=bp, tq=tq, hpb=hpb)

    y_p, hist_p, _, s_p, k_p, v_p = _trunk(
        x_prompt.reshape(bp * seq, d), m0[:bp], m1[:bp], mkv[:bp], wts,
        nseq=bp, tm=tm_p, rows_per_mod=None, dn_nseq=bp, dn_tm=dn_tm_p, dn_chunk=chunk_p,
        s0=jnp.zeros((bp, nh, HEAD_DIM, HEAD_DIM), F32), inj=None, attend=attend_prompt)

    rows_s = db * ROW_GROUP
    pad_t = ROW_GROUP - HIST_ROWS - dseq
    xs = jnp.pad(x_sample, ((0, 0), (HIST_ROWS, pad_t), (0, 0))).reshape(rows_s, d)
    inj = jnp.pad(state_dn_conv[0], ((0, 0), (0, ROW_GROUP - HIST_ROWS), (0, 0))).reshape(rows_s, -1)
    n_pool = cache_k.shape[0]
    ck2 = cache_k.reshape(n_pool * HEAD_DIM * nh, HEAD_DIM)
    cv2 = cache_v.reshape(n_pool * HEAD_DIM * nh, HEAD_DIM)
    bias_s = jnp.repeat(sb_logit_bias[0], ROW_GROUP)[:, None]
    n_pages = page_table.shape[1]
    pps = 8 if n_pages % 8 == 0 else 1

    def attend_sample(qb, k_new, v_new, kb, vb):
        qt = qb.reshape(db, ROW_GROUP, nh, HEAD_DIM).transpose(0, 2, 1, 3)
        qbd = (qt[:, :, :, None, :] * jnp.eye(nh, dtype=BF16)[None, :, None, :, None]).reshape(
            db, nh * ROW_GROUP, width)
        knew = k_new.reshape(db, ROW_GROUP, width)
        vnew = v_new.reshape(db, ROW_GROUP, width)
        o = _sb_decode(qbd, bias_s, knew, vnew, ck2, cv2, page_table, n_pool=n_pool, pages_per_step=pps)
        return o.reshape(db, nh, ROW_GROUP, HEAD_DIM).transpose(0, 2, 1, 3).reshape(rows_s, width)

    tm_s = min(256, rows_s)
    y_s, _, xx_s, s_s, k_s, v_s = _trunk(
        xs, m0[bp:bp + db], m1[bp:bp + db], mkv[bp:bp + db], wts,
        nseq=1, tm=tm_s, rows_per_mod=ROW_GROUP, dn_nseq=db, dn_tm=ROW_GROUP, dn_chunk=ROW_GROUP,
        s0=state_dn_S[0], inj=inj, attend=attend_sample)

    real = slice(HIST_ROWS, HIST_ROWS + dseq)
    grp = lambda a: a.reshape((db, ROW_GROUP) + a.shape[1:])
    return (
        y_p.reshape(bp, seq, d),
        grp(y_s)[:, real],
        hist_p[None, :, ROW_GROUP - HIST_ROWS:],
        s_p[None],
        k_p.reshape(bp, seq, nh, HEAD_DIM),
        v_p.reshape(bp, seq, nh, HEAD_DIM),
        grp(xx_s)[None, :, dseq:dseq + HIST_ROWS],
        s_s[None],
        grp(k_s)[:, real].reshape(db, dseq, nh, HEAD_DIM),
        grp(v_s)[:, real].reshape(db, dseq, nh, HEAD_DIM),
    )
```

```python
import functools
import math

import jax
import jax.numpy as jnp
from jax import lax
from jax.experimental import pallas as pl
from jax.experimental.pallas import tpu as pltpu

F32 = jnp.float32
BF16 = jnp.bfloat16

HEAD_DIM = 128
NORM_EPS = 1e-6
L2_EPS = 1e-6
DN_CONV = 4
HIST_ROWS = DN_CONV - 1
ROW_GROUP = 8
DN_CHUNK = 64
DN_UNIT = 256
INV_BASE = 8
LOG2E = math.log2(math.e)
EXP2_CLAMP = 100.0
V7X_VMEM_LIMIT = 56 * 1024 * 1024


def _params(sem, vmem=V7X_VMEM_LIMIT):
    return pltpu.CompilerParams(dimension_semantics=sem, vmem_limit_bytes=vmem)


def _dot(a, b):
    return jnp.dot(a, b, preferred_element_type=F32)


def _dot_nt(a, b):
    return lax.dot_general(a, b, (((1,), (1,)), ((), ())), preferred_element_type=F32)


def _dot_tn(a, b):
    return lax.dot_general(a, b, (((0,), (0,)), ((), ())), preferred_element_type=F32)


def _split2(x):
    hi = x.astype(BF16)
    lo = (x - hi.astype(F32)).astype(BF16)
    return hi, lo


def _dot01_l(m01, x):
    hi, lo = _split2(x)
    return _dot(m01, hi) + _dot(m01, lo)


def _dot01_r(x, m01):
    hi, lo = _split2(x)
    return _dot(hi, m01) + _dot(lo, m01)


def _sigmoid(x):
    return 1.0 / (1.0 + jnp.exp(-x))


def _silu(x):
    return x * _sigmoid(x)


def _softplus(x):
    return jnp.maximum(x, 0.0) + jnp.log(1.0 + jnp.exp(-jnp.abs(x)))


def _softplus2(z):
    e = jnp.exp2(jnp.minimum(z, EXP2_CLAMP))
    return jnp.maximum(jnp.log(1.0 + e) * LOG2E, z)


def _rms_scale(x):
    return lax.rsqrt(jnp.mean(x * x, axis=-1, keepdims=True) + NORM_EPS)


def _const_spec(shape):
    nd = len(shape)
    return pl.BlockSpec(shape, lambda *_: (0,) * nd, pipeline_mode=pl.Buffered(1))


def _mod_spec(arr, tm):
    if arr.shape[1] == 1:
        return pl.BlockSpec((1, 1, arr.shape[2]), lambda b, i: (b, 0, 0))
    return pl.BlockSpec((1, tm, arr.shape[2]), lambda b, i: (0, i, 0))


def _ada_kernel(c_ref, w_ref, b_ref, o_ref):
    c = _silu(c_ref[...])
    c_hi, c_lo = _split2(c)
    w_hi, w_lo = _split2(w_ref[...])
    o_ref[...] = _dot(c_hi, w_hi) + _dot(c_lo, w_hi) + _dot(c_hi, w_lo) + b_ref[...]


def _ada(c, w, b):
    r, d = c.shape
    n = w.shape[1]
    tn = 1024
    return pl.pallas_call(
        _ada_kernel,
        grid=(n // tn,),
        in_specs=[pl.BlockSpec((r, d), lambda j: (0, 0)),
                  pl.BlockSpec((d, tn), lambda j: (0, j)),
                  pl.BlockSpec((1, tn), lambda j: (0, j))],
        out_specs=pl.BlockSpec((r, tn), lambda j: (0, j)),
        out_shape=jax.ShapeDtypeStruct((r, n), F32),
        compiler_params=_params(("arbitrary",)),
        name="ada",
    )(c, w, b.reshape(1, n))


def _dn_in_kernel(x_ref, sc_ref, sh_ref, g_ref, w_ref, wbg_ref, cw_ref, ap_ref, *rest,
                  tm, width, grouped):
    if grouped:
        inj_ref, q_ref, k_ref, v_ref, z_ref, bg_ref, hist_ref, xxo_ref, xx_ref = rest
    else:
        q_ref, k_ref, v_ref, z_ref, bg_ref, hist_ref, xx_ref = rest
    i = pl.program_id(1)
    x = x_ref[...]
    h = (x * _rms_scale(x) * g_ref[...] * (1.0 + sc_ref[0]) + sh_ref[0]).astype(BF16)

    @pl.when(i == 0)
    def _():
        xx_ref[0:ROW_GROUP, :] = jnp.zeros((ROW_GROUP, 3 * width), F32)

    for c in range(3):
        cs = slice(c * width, (c + 1) * width)
        p = _dot(h, w_ref[:, cs])
        if grouped:
            rr = lax.broadcasted_iota(jnp.int32, (tm, 1), 0) % ROW_GROUP
            m = (rr < HIST_ROWS).astype(F32)
            p = inj_ref[:, cs] * m + p * (1.0 - m)
            xxo_ref[:, cs] = p
        xx_ref[ROW_GROUP:ROW_GROUP + tm, cs] = p
    z_ref[...] = _dot(h, w_ref[:, 3 * width:4 * width])

    nh = width // HEAD_DIM
    bgl = _dot(h, wbg_ref[...])
    lane = lax.broadcasted_iota(jnp.int32, bgl.shape, 1)
    beta = _sigmoid(bgl)
    gdec = ap_ref[0:1, :] * _softplus(bgl + ap_ref[1:2, :])
    bg = jnp.where(lane < nh, beta, jnp.where(lane < 2 * nh, gdec, 0.0))
    if grouped:
        rr = lax.broadcasted_iota(jnp.int32, (tm, 1), 0) % ROW_GROUP
        active = jnp.logical_and(rr >= HIST_ROWS, rr < ROW_GROUP - 1).astype(F32)
        bg = bg * active
    bg_ref[...] = bg

    outs = (q_ref, k_ref, v_ref)
    for c in range(3):
        for hd in range(nh):
            cs = slice(c * width + hd * HEAD_DIM, c * width + (hd + 1) * HEAD_DIM)
            full = xx_ref[0:ROW_GROUP + tm, cs]
            y = full[ROW_GROUP:] * cw_ref[DN_CONV - 1:DN_CONV, cs]
            for back in range(1, DN_CONV):
                shifted = pltpu.roll(full, back, axis=0)[ROW_GROUP:]
                y = y + shifted * cw_ref[DN_CONV - 1 - back:DN_CONV - back, cs]
            y = _silu(y)
            if c < 2:
                norm = lax.rsqrt(jnp.sum(y * y, axis=-1, keepdims=True) + L2_EPS)
                y = y * (norm * (HEAD_DIM ** -0.5) if c == 0 else norm)
            outs[c][:, hd * HEAD_DIM:(hd + 1) * HEAD_DIM] = y

    tail = xx_ref[tm:tm + ROW_GROUP, :]
    hist_ref[0] = tail
    xx_ref[0:ROW_GROUP, :] = tail


def _dn_in(x, sc, sh, g, w, wbg, cw, ap, inj, *, nseq, tm):
    t, d = x.shape
    width = w.shape[1] // 4
    nt = t // (nseq * tm)
    grouped = inj is not None
    row = lambda b, i: (b * nt + i, 0)
    in_specs = [pl.BlockSpec((tm, d), row), _mod_spec(sc, tm), _mod_spec(sh, tm),
                _const_spec((1, d)), _const_spec(w.shape), _const_spec(wbg.shape),
                _const_spec(cw.shape), _const_spec(ap.shape)]
    args = [x, sc, sh, g, w, wbg, cw, ap]
    out_shape = [jax.ShapeDtypeStruct((t, width), F32)] * 4 + [
        jax.ShapeDtypeStruct((t, HEAD_DIM), F32),
        jax.ShapeDtypeStruct((nseq, ROW_GROUP, 3 * width), F32)]
    out_specs = [pl.BlockSpec((tm, width), row)] * 4 + [
        pl.BlockSpec((tm, HEAD_DIM), row),
        pl.BlockSpec((1, ROW_GROUP, 3 * width), lambda b, i: (b, 0, 0))]
    if grouped:
        in_specs.append(pl.BlockSpec((tm, 3 * width), row))
        args.append(inj)
        out_shape.append(jax.ShapeDtypeStruct((t, 3 * width), F32))
        out_specs.append(pl.BlockSpec((tm, 3 * width), row))
    return pl.pallas_call(
        functools.partial(_dn_in_kernel, tm=tm, width=width, grouped=grouped),
        grid=(nseq, nt),
        in_specs=in_specs,
        out_specs=out_specs,
        out_shape=out_shape,
        scratch_shapes=[pltpu.VMEM((tm + ROW_GROUP, 3 * width), F32)],
        compiler_params=_params(("arbitrary", "arbitrary")),
        name="dn_in",
    )(*args)


def _delta_group(heads, rows, u, refs, consts, *, chunk, unit, nh, use_t):
    (q_ref, k_ref, v_ref, z_ref, og_ref, o_ref, s_scr, bg_ref, gcum_scr, grev_scr, gt_scr, mk_ref) = refs

    def spread(ref, lane):
        return jnp.broadcast_to(ref[rows, lane:lane + 1], (unit, HEAD_DIM))

    n_merge, nlev = consts
    hsl = [slice(h * HEAD_DIM, (h + 1) * HEAD_DIM) for h in heads]
    idx = range(len(heads))
    m_strict = mk_ref[0]
    m_incl = mk_ref[1]
    eye = mk_ref[3 + n_merge]

    k = [k_ref[rows, hs] for hs in hsl]
    q = [q_ref[rows, hs] for hs in hsl]
    b = [spread(bg_ref, h) for h in heads]
    gc = [spread(gcum_scr, nh + h) for h in heads]
    k_bf = [x.astype(BF16) for x in k]
    kb = [k[j] * b[j] for j in idx]
    kk = [_dot_nt(kb[j].astype(BF16), k_bf[j]) for j in idx]
    qkr = [_dot_nt(q[j].astype(BF16), k_bf[j]) for j in idx]
    if use_t:
        gi = [jnp.concatenate([gc[j]] * (unit // HEAD_DIM), axis=1) for j in idx]
        gj = [gt_scr[u, pl.ds(nh + h, 1), :] for h in heads]
    else:
        gi = [gc[j][:, :unit] for j in idx]
        ones = jnp.ones((unit, unit), BF16)
        gj = [_dot01_l(ones, gi[j] * eye) for j in idx]
    dec = [jnp.exp(jnp.minimum(gi[j] - gj[j], 0.0)) for j in idx]
    a = [kk[j] * dec[j] * m_strict for j in idx]
    qk = [(qkr[j] * dec[j] * m_incl).astype(BF16) for j in idx]

    m_base = mk_ref[2]
    d_bf = [(a[j] * m_base).astype(BF16) for j in idx]
    p = [_dot(d_bf[j], d_bf[j]) for j in idx]
    r = [-(a[j] * m_base) for j in idx]
    for lev in range(nlev):
        p_bf = [x.astype(BF16) for x in p]
        r = [r[j] + p[j] + _dot(r[j].astype(BF16), p_bf[j]) for j in idx]
        if lev + 1 < nlev:
            p = [_dot(p_bf[j], p_bf[j]) for j in idx]
    for lvl in range(n_merge):
        m_l = mk_ref[3 + lvl]
        t_bf = [(eye + r[j]).astype(BF16) for j in idx]
        x = [_dot(t_bf[j], (a[j] * m_l).astype(BF16)) for j in idx]
        r = [r[j] - _dot(x[j].astype(BF16), t_bf[j]) for j in idx]

    eg = [jnp.exp(gc[j]) for j in idx]
    rhs = [jnp.concatenate([v_ref[rows, hsl[j]] * b[j], kb[j] * eg[j]], axis=1) for j in idx]
    uw = [rhs[j] + _dot(r[j].astype(BF16), rhs[j].astype(BF16)) for j in idx]
    qd = [q[j] * eg[j] for j in idx]
    kd = [k[j] * jnp.exp(spread(grev_scr, nh + h)) for j, h in enumerate(heads)]

    s = [s_scr[h] for h in heads]
    vn = [[] for _ in idx]
    oq = [[] for _ in idx]
    for c in range(unit // chunk):
        cs = slice(c * chunk, (c + 1) * chunk)
        wq = [_dot(jnp.concatenate([uw[j][cs, HEAD_DIM:], qd[j][cs]], axis=0).astype(BF16),
                   s[j].astype(BF16)) for j in idx]
        for j in idx:
            vn[j].append((uw[j][cs, :HEAD_DIM] - wq[j][:chunk]).astype(BF16))
            oq[j].append(wq[j][chunk:])
        last = (c + 1) * chunk - 1
        s = [s[j] * jnp.exp(gc[j][last:last + 1, :]) + _dot_tn(kd[j][cs].astype(BF16), vn[j][-1])
             for j in idx]
    for j, h in enumerate(heads):
        s_scr[h] = s[j]
    for j in idx:
        o = jnp.concatenate(oq[j], axis=0) + _dot(qk[j], jnp.concatenate(vn[j], axis=0))
        on = o * lax.rsqrt(jnp.mean(o * o, axis=-1, keepdims=True) + NORM_EPS) * og_ref[...]
        o_ref[rows, hsl[j]] = (on * _silu(z_ref[rows, hsl[j]])).astype(o_ref.dtype)


def _delta_kernel(q_ref, k_ref, v_ref, z_ref, bg_ref, s0_ref, og_ref,
                  ltri_ref, urev_ref, ltrit_ref, mk_ref,
                  o_ref, sout_ref, s_scr, gcum_scr, grev_scr, gt_scr,
                  *, tm, chunk, nh, unit, hgroup):
    i = pl.program_id(1)
    n_units = tm // unit
    base = min(INV_BASE, chunk)
    nlev = max(base.bit_length() - 2, 0)
    n_merge = mk_ref.shape[0] - 4
    use_t = unit % HEAD_DIM == 0

    @pl.when(i == 0)
    def _():
        s_scr[...] = s0_ref[0]

    bg = bg_ref[...]
    if use_t:
        bgt = bg.T
    for u in range(n_units):
        rs = slice(u * unit, (u + 1) * unit)
        g_hi, g_lo = _split2(bg[rs, :])
        gcum_scr[rs, :] = _dot(ltri_ref[...], g_hi) + _dot(ltri_ref[...], g_lo)
        grev_scr[rs, :] = _dot(urev_ref[...], g_hi) + _dot(urev_ref[...], g_lo)
        if use_t:
            gt_scr[u] = _dot01_r(bgt[:, rs], ltrit_ref[...])

    refs = (q_ref, k_ref, v_ref, z_ref, og_ref, o_ref, s_scr, bg_ref, gcum_scr, grev_scr, gt_scr, mk_ref)

    def unit_body(u, carry):
        rows = pl.ds(pl.multiple_of(u * unit, unit), unit)
        for g0 in range(0, nh, hgroup):
            _delta_group(list(range(g0, g0 + hgroup)), rows, u, refs, (n_merge, nlev),
                         chunk=chunk, unit=unit, nh=nh, use_t=use_t)
        return carry

    lax.fori_loop(0, n_units, unit_body, 0)

    @pl.when(i == pl.num_programs(1) - 1)
    def _():
        sout_ref[0] = s_scr[...]


def _delta(q, k, v, z, bg, s0, og, *, nseq, tm, chunk):
    t, width = q.shape
    nh = width // HEAD_DIM
    nt = t // (nseq * tm)
    unit = min(DN_UNIT, tm)
    assert tm % unit == 0 and unit % chunk == 0
    row = lambda b, i: (b * nt + i, 0)
    ri = jnp.arange(unit)[:, None]
    ci = jnp.arange(unit)[None, :]
    same = (ri // chunk) == (ci // chunk)
    ltri = jnp.logical_and(same, ri >= ci)
    urev = jnp.logical_and(same, ci > ri)
    base = min(INV_BASE, chunk)
    masks = [jnp.logical_and(same, ri > ci), ltri, (ri // base) == (ci // base)]
    m = base
    while m < chunk:
        masks.append(jnp.logical_and(ri // (2 * m) == ci // (2 * m), ri // m != ci // m))
        m *= 2
    masks.append(ri == ci)
    mk = jnp.stack(masks).astype(F32)
    ltri_b = ltri.astype(BF16)
    return pl.pallas_call(
        functools.partial(_delta_kernel, tm=tm, chunk=chunk, nh=nh, unit=unit, hgroup=nh),
        grid=(nseq, nt),
        in_specs=[pl.BlockSpec((tm, width), row)] * 4 + [
            pl.BlockSpec((tm, HEAD_DIM), row),
            pl.BlockSpec((1, nh, HEAD_DIM, HEAD_DIM), lambda b, i: (b, 0, 0, 0)),
            _const_spec((1, HEAD_DIM)),
            _const_spec((unit, unit)), _const_spec((unit, unit)), _const_spec((unit, unit)),
            _const_spec(mk.shape)],
        out_specs=[pl.BlockSpec((tm, width), row),
                   pl.BlockSpec((1, nh, HEAD_DIM, HEAD_DIM), lambda b, i: (b, 0, 0, 0))],
        out_shape=[jax.ShapeDtypeStruct((t, width), BF16 if tm % 16 == 0 else F32),
                   jax.ShapeDtypeStruct((nseq, nh, HEAD_DIM, HEAD_DIM), F32)],
        scratch_shapes=[pltpu.VMEM((nh, HEAD_DIM, HEAD_DIM), F32),
                        pltpu.VMEM((tm, HEAD_DIM), F32), pltpu.VMEM((tm, HEAD_DIM), F32),
                        pltpu.VMEM((tm // unit, HEAD_DIM, max(unit, HEAD_DIM)), F32)],
        compiler_params=_params(("arbitrary", "arbitrary")),
        name="delta",
    )(q, k, v, z, bg, s0, og, ltri_b, urev.astype(BF16), ltri_b.T, mk)


def _mixffn_kernel(mix_ref, x_ref, wo_ref, g1_ref, gt1_ref, g2_ref, sc2_ref, sh2_ref,
                   wi_ref, wout_ref, g3_ref, gt2_ref, y_ref, *, hidden, hchunk):
    a = _dot(mix_ref[...].astype(BF16), wo_ref[...])
    x1 = x_ref[...] + gt1_ref[0] * (a * _rms_scale(a) * g1_ref[...])
    h = (x1 * _rms_scale(x1) * g2_ref[...] * (1.0 + sc2_ref[0]) + sh2_ref[0]).astype(BF16)
    f = None
    for c0 in range(0, hidden, hchunk):
        gate = _dot(h, wi_ref[:, c0:c0 + hchunk])
        up = _dot(h, wi_ref[:, hidden + c0:hidden + c0 + hchunk])
        part = _dot((_silu(gate) * up).astype(BF16), wout_ref[c0:c0 + hchunk, :])
        f = part if f is None else f + part
    y_ref[...] = x1 + gt2_ref[0] * (f * _rms_scale(f) * g3_ref[...])


def _mixffn(mix, x, wo, g1, gt1, g2, sc2, sh2, wi, wout, g3, gt2, *, nseq, tm):
    t, d = x.shape
    nt = t // (nseq * tm)
    hidden = wout.shape[0]
    hchunk = hidden // 4
    row = lambda b, i: (b * nt + i, 0)
    vec = _const_spec((1, d))
    return pl.pallas_call(
        functools.partial(_mixffn_kernel, hidden=hidden, hchunk=hchunk),
        grid=(nseq, nt),
        in_specs=[pl.BlockSpec((tm, mix.shape[1]), row), pl.BlockSpec((tm, d), row),
                  _const_spec(wo.shape), vec, _mod_spec(gt1, tm), vec, _mod_spec(sc2, tm),
                  _mod_spec(sh2, tm), _const_spec(wi.shape), _const_spec(wout.shape), vec,
                  _mod_spec(gt2, tm)],
        out_specs=pl.BlockSpec((tm, d), row),
        out_shape=jax.ShapeDtypeStruct((t, d), F32),
        compiler_params=_params(("arbitrary", "arbitrary")),
        name="mixffn",
    )(mix, x, wo, g1, gt1, g2, sc2, sh2, wi, wout, g3, gt2)


def _attn_in_kernel(x_ref, gq_ref, scq_ref, shq_ref, gk_ref, sck_ref, shk_ref, wq_ref, wkv_ref,
                    q_ref, k_ref, v_ref, kb_ref, vb_ref, *, width):
    x = x_ref[...]
    xn = x * _rms_scale(x)
    hq = (xn * gq_ref[...] * (1.0 + scq_ref[0]) + shq_ref[0]).astype(BF16)
    hk = (xn * gk_ref[...] * (1.0 + sck_ref[0]) + shk_ref[0]).astype(BF16)
    q_ref[...] = (_dot(hq, wq_ref[...]) * (LOG2E * HEAD_DIM ** -0.5)).astype(BF16)
    k = _dot(hk, wkv_ref[:, :width])
    v = _dot(hk, wkv_ref[:, width:])
    k_ref[...] = k
    v_ref[...] = v
    kb_ref[...] = k.astype(BF16)
    vb_ref[...] = v.astype(BF16)


def _attn_in(x, gq, scq, shq, gk, sck, shk, wq, wkv, *, nseq, tm):
    t, d = x.shape
    width = wq.shape[1]
    nt = t // (nseq * tm)
    row = lambda b, i: (b * nt + i, 0)
    vec = _const_spec((1, d))
    blk = pl.BlockSpec((tm, width), row)
    return pl.pallas_call(
        functools.partial(_attn_in_kernel, width=width),
        grid=(nseq, nt),
        in_specs=[pl.BlockSpec((tm, d), row), vec, _mod_spec(scq, tm), _mod_spec(shq, tm),
                  vec, _mod_spec(sck, tm), _mod_spec(shk, tm),
                  _const_spec(wq.shape), _const_spec(wkv.shape)],
        out_specs=[blk] * 5,
        out_shape=[jax.ShapeDtypeStruct((t, width), BF16), jax.ShapeDtypeStruct((t, width), F32),
                   jax.ShapeDtypeStruct((t, width), F32), jax.ShapeDtypeStruct((t, width), BF16),
                   jax.ShapeDtypeStruct((t, width), BF16)],
        compiler_params=_params(("arbitrary", "arbitrary")),
        name="attn_in",
    )(x, gq, scq, shq, gk, sck, shk, wq, wkv)


def _causal_mask(shape):
    ri = lax.broadcasted_iota(jnp.int32, shape, 0)
    ci = lax.broadcasted_iota(jnp.int32, shape, 1)
    return ci < ri


def _sb_prompt_kernel(bias_ref, tri_ref, q_ref, k_ref, v_ref, o_ref, z_scr, acc_scr, carry_scr,
                      *, tq, hpb):
    hg = pl.program_id(1)
    i = pl.program_id(2)
    heads = range(hpb)
    hsl = [slice(j * HEAD_DIM, (j + 1) * HEAD_DIM) for j in heads]
    qs = [q_ref[:, hs] for hs in hsl]
    biases = [bias_ref[hg * hpb + j] * LOG2E for j in heads]

    def rows(b):
        return pl.ds(pl.multiple_of(jnp.maximum(i - b, 0) * tq, tq), tq)

    def logits(b, slot):
        r = rows(b)
        for j in heads:
            z_scr[slot, j] = _dot_nt(qs[j], k_ref[r, hsl[j]]) + biases[j]

    def consume(b, slot, first):
        r = rows(b)
        lsig, sp_bf = [], []
        for j in heads:
            z = z_scr[slot, j]
            sp = _softplus2(z)
            if first:
                mask = _causal_mask(z.shape)
                lsig.append(z - sp)
                sp = jnp.where(mask, sp, 0.0)
                carry_scr[j] = jnp.sum(sp, axis=-1, keepdims=True)
            else:
                carry = carry_scr[j]
                lsig.append(z - sp - carry)
                carry_scr[j] = carry + jnp.sum(sp, axis=-1, keepdims=True)
            sp_bf.append(sp.astype(BF16))
        att = [jnp.exp2(lsig[j] - _dot(sp_bf[j], tri_ref[...])) for j in heads]
        if first:
            att = [jnp.where(mask, att[j], 0.0) for j in heads]
        for j in heads:
            contrib = _dot(att[j].astype(BF16), v_ref[r, hsl[j]])
            acc_scr[j] = contrib if first else acc_scr[j] + contrib

    logits(0, 0)
    logits(1, 1)
    consume(0, 0, True)

    def pair(p, c):
        b = 2 * p + 1
        logits(b + 1, 0)
        consume(b, 1, False)
        logits(b + 2, 1)
        consume(b + 1, 0, False)
        return c

    lax.fori_loop(0, i // 2, pair, 0)

    @pl.when(i % 2 == 1)
    def _():
        consume(i, 1, False)

    for j in heads:
        o_ref[:, hsl[j]] = acc_scr[j].astype(o_ref.dtype)


def _sb_prompt(q, kb, vb, bias, *, nseq, tq, hpb):
    t, width = q.shape
    nh = width // HEAD_DIM
    seq = t // nseq
    nq = seq // tq
    ri = jnp.arange(tq)[:, None]
    ci = jnp.arange(tq)[None, :]
    tri = (ri > ci).astype(BF16)
    wblk = hpb * HEAD_DIM
    return pl.pallas_call(
        functools.partial(_sb_prompt_kernel, tq=tq, hpb=hpb),
        grid=(nseq, nh // hpb, nq),
        in_specs=[pl.BlockSpec(memory_space=pltpu.SMEM),
                  _const_spec((tq, tq)),
                  pl.BlockSpec((tq, wblk), lambda b, h, i: (b * nq + i, h)),
                  pl.BlockSpec((seq, wblk), lambda b, h, i: (b, h)),
                  pl.BlockSpec((seq, wblk), lambda b, h, i: (b, h))],
        out_specs=pl.BlockSpec((tq, wblk), lambda b, h, i: (b * nq + i, h)),
        out_shape=jax.ShapeDtypeStruct((t, width), BF16),
        scratch_shapes=[pltpu.VMEM((2, hpb, tq, tq), F32), pltpu.VMEM((hpb, tq, HEAD_DIM), F32),
                        pltpu.VMEM((hpb, tq, 1), F32)],
        compiler_params=_params(("arbitrary", "arbitrary", "arbitrary")),
        name="sb_prompt",
    )(bias, tri, q, kb, vb)


def _dec_blocks(kcats, vcats, qbd, bias, carry, keymask):
    nb = range(len(kcats))
    z = [_dot_nt(qbd, kcats[p]) + bias for p in nb]
    sp = [_softplus2(z[p]) for p in nb]
    if keymask is not None:
        sp = [jnp.where(keymask, sp[p], 0.0) for p in nb]
    n = z[0].shape[1]
    ri = lax.broadcasted_iota(jnp.int32, (n, n), 0)
    ci = lax.broadcasted_iota(jnp.int32, (n, n), 1)
    mrev = (ri > ci).astype(BF16)
    within = [_dot(sp[p].astype(BF16), mrev) for p in nb]
    out = None
    for p in nb:
        att = jnp.exp2(z[p] - sp[p] - within[p] - carry)
        if keymask is not None:
            att = jnp.where(keymask, att, 0.0)
        contrib = _dot(att.astype(BF16), vcats[p])
        out = contrib if out is None else out + contrib
        carry = carry + jnp.sum(sp[p], axis=1, keepdims=True)
    return out, carry


def _sb_decode_kernel(pt_ref, qbd_ref, bias_ref, knew_ref, vnew_ref, *rest, pages_per_step, nh):
    del pt_ref
    pp = pages_per_step
    k_refs = rest[:pp]
    v_refs = rest[pp:2 * pp]
    o_ref, acc_ref, carry_ref = rest[2 * pp:]
    j = pl.program_id(1)
    qbd = qbd_ref[0]
    bias = bias_ref[...] * LOG2E
    w = qbd.shape[0]

    @pl.when(j == 0)
    def _():
        s_idx = lax.broadcasted_iota(jnp.int32, (w, ROW_GROUP), 1)
        i_idx = lax.broadcasted_iota(jnp.int32, (w, ROW_GROUP), 0) % ROW_GROUP
        real = jnp.logical_and(s_idx >= HIST_ROWS, s_idx < ROW_GROUP - 1)
        keymask = jnp.logical_and(real, s_idx < i_idx)
        contrib, carry = _dec_blocks([knew_ref[0].astype(BF16)], [vnew_ref[0].astype(BF16)], qbd, bias,
                                     jnp.zeros((w, 1), F32), keymask)
        acc_ref[...] = contrib
        carry_ref[...] = jnp.broadcast_to(carry, carry_ref.shape)

    n_tok = k_refs[0].shape[0] // nh

    def cat(ref):
        return jnp.concatenate([ref[pl.ds(h, n_tok, stride=nh), :].astype(BF16) for h in range(nh)], axis=1)

    contrib, carry = _dec_blocks([cat(r) for r in k_refs], [cat(r) for r in v_refs], qbd, bias,
                                 carry_ref[:, 0:1], None)
    acc = acc_ref[...] + contrib
    acc_ref[...] = acc
    carry_ref[...] = jnp.broadcast_to(carry, carry_ref.shape)

    @pl.when(j == pl.num_programs(1) - 1)
    def _():
        o_ref[0] = jnp.concatenate(
            [acc[h * ROW_GROUP:(h + 1) * ROW_GROUP, h * HEAD_DIM:(h + 1) * HEAD_DIM] for h in range(nh)],
            axis=0).astype(o_ref.dtype)


def _sb_decode(qbd, bias, knew, vnew, cache_k2, cache_v2, page_table, *, n_pool, pages_per_step):
    nb, n_pages = page_table.shape
    pp = pages_per_step
    steps = n_pages // pp
    w, width = qbd.shape[1:]
    nh = width // HEAD_DIM
    page_rows = cache_k2.shape[0] // n_pool

    def page_spec(r):
        def imap(b, j, pt):
            return (pt[b * n_pages + (n_pages - 1 - (j * pp + r))], 0)
        return pl.BlockSpec((page_rows, HEAD_DIM), imap)

    seq3 = lambda b, j, pt: (b, 0, 0)
    grid_spec = pltpu.PrefetchScalarGridSpec(
        num_scalar_prefetch=1,
        grid=(nb, steps),
        in_specs=[pl.BlockSpec((1, w, width), seq3),
                  pl.BlockSpec((w, 1), lambda b, j, pt: (0, 0)),
                  pl.BlockSpec((1, ROW_GROUP, width), seq3),
                  pl.BlockSpec((1, ROW_GROUP, width), seq3)]
        + [page_spec(r) for r in range(pp)] * 2,
        out_specs=pl.BlockSpec((1, w, HEAD_DIM), seq3),
        scratch_shapes=[pltpu.VMEM((w, width), F32), pltpu.VMEM((w, HEAD_DIM), F32)],
    )
    return pl.pallas_call(
        functools.partial(_sb_decode_kernel, pages_per_step=pp, nh=nh),
        grid_spec=grid_spec,
        out_shape=jax.ShapeDtypeStruct((nb, w, HEAD_DIM), BF16),
        compiler_params=_params(("arbitrary", "arbitrary")),
        name="sb_decode",
    )(page_table.reshape(-1), qbd, bias, knew, vnew,
      *([cache_k2] * pp), *([cache_v2] * pp))


def _mods(m, n, rows):
    parts = jnp.split(m, n, axis=-1)
    if rows is None:
        return [p[:, None, :] for p in parts]
    return [jnp.repeat(p, rows, axis=0)[None] for p in parts]


def _trunk(x2d, m0, m1, mkv, wts, *, nseq, tm, ffn_tm, rows_per_mod, dn_nseq, dn_tm, dn_chunk, s0, inj,
           attend):
    sh1, sc1, gt1, sh2, sc2, gt2 = _mods(m0, 6, rows_per_mod)
    dn = _dn_in(x2d, sc1, sh1, wts["mix_pre_g"][0], wts["dn_w_qkvz"], wts["dn_w_bg"],
                wts["dn_conv_w"], wts["dn_ap"], inj, nseq=nseq, tm=tm)
    q, k, v, z, bg, hist = dn[:6]
    xx = dn[6] if inj is not None else None
    o_dn, s_new = _delta(q, k, v, z, bg, s0, wts["dn_onorm_g"], nseq=dn_nseq, tm=dn_tm,
                         chunk=dn_chunk)
    x2 = _mixffn(o_dn, x2d, wts["dn_w_out"], wts["mix_post_g"][0], gt1, wts["ffn_pre_g"][0], sc2, sh2,
                 wts["ffn_w_in"][0], wts["ffn_w_out"][0], wts["ffn_post_g"][0], gt2, nseq=nseq, tm=ffn_tm)

    sh1, sc1, gt1, sh2, sc2, gt2 = _mods(m1, 6, rows_per_mod)
    ksh, ksc = _mods(mkv, 2, rows_per_mod)
    qb, k_new, v_new, kb, vb = _attn_in(x2, wts["mix_pre_g"][1], sc1, sh1, wts["kv_norm_g"], ksc, ksh,
                                        wts["sb_w_q"], wts["sb_w_kv"], nseq=nseq, tm=tm)
    o_sb = attend(qb, k_new, v_new, kb, vb)
    y = _mixffn(o_sb, x2, wts["sb_w_o"], wts["mix_post_g"][1], gt1, wts["ffn_pre_g"][1], sc2, sh2,
                wts["ffn_w_in"][1], wts["ffn_w_out"][1], wts["ffn_post_g"][1], gt2, nseq=nseq, tm=ffn_tm)
    return y, hist, xx, s_new, k_new, v_new


def kernel(x_prompt, x_sample, c_prompt, c_sample, state_dn_conv, state_dn_S, cache_k, cache_v, page_table, ada_w, ada_b, mix_pre_g, mix_post_g, ffn_pre_g, ffn_post_g, ffn_w_in, ffn_w_out, dn_w_in, dn_conv_w, dn_a_log, dn_dt_bias, dn_onorm_g, dn_w_out, kv_ada_w, kv_ada_b, kv_norm_g, sb_w_kv, sb_w_q, sb_logit_bias, sb_w_o):
    bp, seq, d = x_prompt.shape
    db, dseq, _ = x_sample.shape
    nh = d // HEAD_DIM
    width = nh * HEAD_DIM
    assert ada_w.shape[0] == 2 and dn_w_in.shape[0] == 1 and sb_w_q.shape[0] == 1
    assert dseq == ROW_GROUP - HIST_ROWS - 1 and cache_k.shape[1] == HEAD_DIM and cache_k.shape[2] == nh

    vec = lambda g: g.reshape(g.shape[0], 1, g.shape[1])
    w_in = dn_w_in[0]
    pad_lanes = HEAD_DIM - 2 * nh
    zpad = jnp.zeros((nh,), F32)
    lane_pad = lambda a: jnp.concatenate([zpad, a, jnp.zeros((pad_lanes,), F32)])
    wts = {
        "mix_pre_g": vec(mix_pre_g), "mix_post_g": vec(mix_post_g),
        "ffn_pre_g": vec(ffn_pre_g), "ffn_post_g": vec(ffn_post_g),
        "ffn_w_in": ffn_w_in.astype(BF16), "ffn_w_out": ffn_w_out.astype(BF16),
        "dn_w_qkvz": w_in[:, :4 * width].astype(BF16),
        "dn_w_bg": jnp.pad(w_in[:, 4 * width:], ((0, 0), (0, pad_lanes))).astype(BF16),
        "dn_conv_w": dn_conv_w[0],
        "dn_ap": jnp.stack([lane_pad(-jnp.exp(dn_a_log[0])), lane_pad(dn_dt_bias[0])]),
        "dn_onorm_g": dn_onorm_g,
        "dn_w_out": dn_w_out[0].astype(BF16),
        "kv_norm_g": kv_norm_g.reshape(1, d),
        "sb_w_kv": sb_w_kv.astype(BF16), "sb_w_q": sb_w_q[0].astype(BF16),
        "sb_w_o": sb_w_o[0].astype(BF16),
    }

    c_all = jnp.concatenate([c_prompt, c_sample], axis=0)
    c_all = jnp.pad(c_all, ((0, -(bp + db) % 16), (0, 0)))
    m0 = _ada(c_all, ada_w[0], ada_b[0])
    m1 = _ada(c_all, ada_w[1], ada_b[1])
    mkv = _ada(c_all, kv_ada_w, kv_ada_b)

    tm_p = min(256, seq)
    chunk_p = min(DN_CHUNK, seq)
    dn_tm_p = min(512, seq)
    tq = min(256, seq)
    hpb = min(4, nh)
    bias_p = sb_logit_bias[0]

    def attend_prompt(qb, k_new, v_new, kb, vb):
        return _sb_prompt(qb, kb, vb, bias_p, nseq=bp, tq=tq, hpb=hpb)

    y_p, hist_p, _, s_p, k_p, v_p = _trunk(
        x_prompt.reshape(bp * seq, d), m0[:bp], m1[:bp], mkv[:bp], wts,
        nseq=bp, tm=tm_p, ffn_tm=min(512, seq), rows_per_mod=None, dn_nseq=bp, dn_tm=dn_tm_p, dn_chunk=chunk_p,
        s0=jnp.zeros((bp, nh, HEAD_DIM, HEAD_DIM), F32), inj=None, attend=attend_prompt)

    rows_s = db * ROW_GROUP
    pad_t = ROW_GROUP - HIST_ROWS - dseq
    xs = jnp.pad(x_sample, ((0, 0), (HIST_ROWS, pad_t), (0, 0))).reshape(rows_s, d)
    inj = jnp.pad(state_dn_conv[0], ((0, 0), (0, ROW_GROUP - HIST_ROWS), (0, 0))).reshape(rows_s, -1)
    n_pool = cache_k.shape[0]
    ck2 = cache_k.reshape(n_pool * HEAD_DIM * nh, HEAD_DIM)
    cv2 = cache_v.reshape(n_pool * HEAD_DIM * nh, HEAD_DIM)
    bias_s = jnp.repeat(sb_logit_bias[0], ROW_GROUP)[:, None]
    n_pages = page_table.shape[1]
    pps = 16 if n_pages % 16 == 0 else 1

    def attend_sample(qb, k_new, v_new, kb, vb):
        qt = qb.reshape(db, ROW_GROUP, nh, HEAD_DIM).transpose(0, 2, 1, 3)
        qbd = (qt[:, :, :, None, :] * jnp.eye(nh, dtype=BF16)[None, :, None, :, None]).reshape(
            db, nh * ROW_GROUP, width)
        knew = k_new.reshape(db, ROW_GROUP, width)
        vnew = v_new.reshape(db, ROW_GROUP, width)
        o = _sb_decode(qbd, bias_s, knew, vnew, ck2, cv2, page_table, n_pool=n_pool, pages_per_step=pps)
        return o.reshape(db, nh, ROW_GROUP, HEAD_DIM).transpose(0, 2, 1, 3).reshape(rows_s, width)

    tm_s = min(256, rows_s)
    y_s, _, xx_s, s_s, k_s, v_s = _trunk(
        xs, m0[bp:bp + db], m1[bp:bp + db], mkv[bp:bp + db], wts,
        nseq=1, tm=tm_s, ffn_tm=tm_s, rows_per_mod=ROW_GROUP, dn_nseq=db, dn_tm=ROW_GROUP, dn_chunk=ROW_GROUP,
        s0=state_dn_S[0], inj=inj, attend=attend_sample)

    real = slice(HIST_ROWS, HIST_ROWS + dseq)
    grp = lambda a: a.reshape((db, ROW_GROUP) + a.shape[1:])
    return (
        y_p.reshape(bp, seq, d),
        grp(y_s)[:, real],
        hist_p[None, :, ROW_GROUP - HIST_ROWS:],
        s_p[None],
        k_p.reshape(bp, seq, nh, HEAD_DIM),
        v_p.reshape(bp, seq, nh, HEAD_DIM),
        grp(xx_s)[None, :, dseq:dseq + HIST_ROWS],
        s_s[None],
        grp(k_s)[:, real].reshape(db, dseq, nh, HEAD_DIM),
        grp(v_s)[:, real].reshape(db, dseq, nh, HEAD_DIM),
    )
```

```python
import functools
import math

import jax
import jax.numpy as jnp
from jax import lax
from jax.experimental import pallas as pl
from jax.experimental.pallas import tpu as pltpu

F32 = jnp.float32
BF16 = jnp.bfloat16

HEAD_DIM = 128
NORM_EPS = 1e-6
L2_EPS = 1e-6
DN_CONV = 4
HIST_ROWS = DN_CONV - 1
ROW_GROUP = 8
DN_CHUNK = 64
DN_UNIT = 256
INV_BASE = 8
LOG2E = math.log2(math.e)
EXP2_CLAMP = 100.0
V7X_VMEM_LIMIT = 56 * 1024 * 1024

ROW_TILE = 256
FFN_ROW_TILE = 512
DN_ROW_TILE = 512
SB_BLOCK = 256
SB_HEADS_PER_STEP = 4
DECODE_PAGES_PER_STEP = 16


def _params(sem, vmem=V7X_VMEM_LIMIT):
    return pltpu.CompilerParams(dimension_semantics=sem, vmem_limit_bytes=vmem)


def _dot(a, b):
    return jnp.dot(a, b, preferred_element_type=F32)


def _dot_nt(a, b):
    return lax.dot_general(a, b, (((1,), (1,)), ((), ())), preferred_element_type=F32)


def _dot_tn(a, b):
    return lax.dot_general(a, b, (((0,), (0,)), ((), ())), preferred_element_type=F32)


def _split2(x):
    hi = x.astype(BF16)
    lo = (x - hi.astype(F32)).astype(BF16)
    return hi, lo


def _dot01_l(m01, x):
    hi, lo = _split2(x)
    return _dot(m01, hi) + _dot(m01, lo)


def _dot01_r(x, m01):
    hi, lo = _split2(x)
    return _dot(hi, m01) + _dot(lo, m01)


def _sigmoid(x):
    return 1.0 / (1.0 + jnp.exp(-x))


def _silu(x):
    return x * _sigmoid(x)


def _softplus(x):
    return jnp.maximum(x, 0.0) + jnp.log(1.0 + jnp.exp(-jnp.abs(x)))


def _softplus2(z):
    e = jnp.exp2(jnp.minimum(z, EXP2_CLAMP))
    return jnp.maximum(jnp.log(1.0 + e) * LOG2E, z)


def _rms_scale(x):
    return lax.rsqrt(jnp.mean(x * x, axis=-1, keepdims=True) + NORM_EPS)


def _const_spec(shape):
    nd = len(shape)
    return pl.BlockSpec(shape, lambda *_: (0,) * nd, pipeline_mode=pl.Buffered(1))


def _mod_spec(arr, tm):
    if arr.shape[1] == 1:
        return pl.BlockSpec((1, 1, arr.shape[2]), lambda b, i: (b, 0, 0))
    return pl.BlockSpec((1, tm, arr.shape[2]), lambda b, i: (0, i, 0))


def _ada_kernel(c_ref, w_ref, b_ref, o_ref):
    c = _silu(c_ref[...])
    c_hi, c_lo = _split2(c)
    w_hi, w_lo = _split2(w_ref[...])
    o_ref[...] = _dot(c_hi, w_hi) + _dot(c_lo, w_hi) + _dot(c_hi, w_lo) + b_ref[...]


def _ada(c, w, b):
    r, d = c.shape
    n = w.shape[1]
    tn = 1024
    return pl.pallas_call(
        _ada_kernel,
        grid=(n // tn,),
        in_specs=[pl.BlockSpec((r, d), lambda j: (0, 0)),
                  pl.BlockSpec((d, tn), lambda j: (0, j)),
                  pl.BlockSpec((1, tn), lambda j: (0, j))],
        out_specs=pl.BlockSpec((r, tn), lambda j: (0, j)),
        out_shape=jax.ShapeDtypeStruct((r, n), F32),
        compiler_params=_params(("arbitrary",)),
        name="ada",
    )(c, w, b.reshape(1, n))


def _dn_in_kernel(x_ref, sc_ref, sh_ref, g_ref, w_ref, wbg_ref, cw_ref, ap_ref, *rest,
                  tm, width, grouped):
    if grouped:
        inj_ref, q_ref, k_ref, v_ref, z_ref, bg_ref, hist_ref, xxo_ref, xx_ref = rest
    else:
        q_ref, k_ref, v_ref, z_ref, bg_ref, hist_ref, xx_ref = rest
    i = pl.program_id(1)
    x = x_ref[...]
    h = (x * _rms_scale(x) * g_ref[...] * (1.0 + sc_ref[0]) + sh_ref[0]).astype(BF16)

    @pl.when(i == 0)
    def _():
        xx_ref[0:ROW_GROUP, :] = jnp.zeros((ROW_GROUP, 3 * width), F32)

    for c in range(3):
        cs = slice(c * width, (c + 1) * width)
        p = _dot(h, w_ref[:, cs])
        if grouped:
            rr = lax.broadcasted_iota(jnp.int32, (tm, 1), 0) % ROW_GROUP
            m = (rr < HIST_ROWS).astype(F32)
            p = inj_ref[:, cs] * m + p * (1.0 - m)
            xxo_ref[:, cs] = p
        xx_ref[ROW_GROUP:ROW_GROUP + tm, cs] = p
    z_ref[...] = _dot(h, w_ref[:, 3 * width:4 * width])

    nh = width // HEAD_DIM
    bgl = _dot(h, wbg_ref[...])
    lane = lax.broadcasted_iota(jnp.int32, bgl.shape, 1)
    beta = _sigmoid(bgl)
    gdec = ap_ref[0:1, :] * _softplus(bgl + ap_ref[1:2, :])
    bg = jnp.where(lane < nh, beta, jnp.where(lane < 2 * nh, gdec, 0.0))
    if grouped:
        rr = lax.broadcasted_iota(jnp.int32, (tm, 1), 0) % ROW_GROUP
        active = jnp.logical_and(rr >= HIST_ROWS, rr < ROW_GROUP - 1).astype(F32)
        bg = bg * active
    bg_ref[...] = bg

    outs = (q_ref, k_ref, v_ref)
    for c in range(3):
        for hd in range(nh):
            cs = slice(c * width + hd * HEAD_DIM, c * width + (hd + 1) * HEAD_DIM)
            full = xx_ref[0:ROW_GROUP + tm, cs]
            y = full[ROW_GROUP:] * cw_ref[DN_CONV - 1:DN_CONV, cs]
            for back in range(1, DN_CONV):
                shifted = pltpu.roll(full, back, axis=0)[ROW_GROUP:]
                y = y + shifted * cw_ref[DN_CONV - 1 - back:DN_CONV - back, cs]
            y = _silu(y)
            if c < 2:
                norm = lax.rsqrt(jnp.sum(y * y, axis=-1, keepdims=True) + L2_EPS)
                y = y * (norm * (HEAD_DIM ** -0.5) if c == 0 else norm)
            outs[c][:, hd * HEAD_DIM:(hd + 1) * HEAD_DIM] = y

    tail = xx_ref[tm:tm + ROW_GROUP, :]
    hist_ref[0] = tail
    xx_ref[0:ROW_GROUP, :] = tail


def _dn_in(x, sc, sh, g, w, wbg, cw, ap, inj, *, nseq, tm):
    t, d = x.shape
    width = w.shape[1] // 4
    nt = t // (nseq * tm)
    grouped = inj is not None
    row = lambda b, i: (b * nt + i, 0)
    in_specs = [pl.BlockSpec((tm, d), row), _mod_spec(sc, tm), _mod_spec(sh, tm),
                _const_spec((1, d)), _const_spec(w.shape), _const_spec(wbg.shape),
                _const_spec(cw.shape), _const_spec(ap.shape)]
    args = [x, sc, sh, g, w, wbg, cw, ap]
    out_shape = [jax.ShapeDtypeStruct((t, width), F32)] * 4 + [
        jax.ShapeDtypeStruct((t, HEAD_DIM), F32),
        jax.ShapeDtypeStruct((nseq, ROW_GROUP, 3 * width), F32)]
    out_specs = [pl.BlockSpec((tm, width), row)] * 4 + [
        pl.BlockSpec((tm, HEAD_DIM), row),
        pl.BlockSpec((1, ROW_GROUP, 3 * width), lambda b, i: (b, 0, 0))]
    if grouped:
        in_specs.append(pl.BlockSpec((tm, 3 * width), row))
        args.append(inj)
        out_shape.append(jax.ShapeDtypeStruct((t, 3 * width), F32))
        out_specs.append(pl.BlockSpec((tm, 3 * width), row))
    return pl.pallas_call(
        functools.partial(_dn_in_kernel, tm=tm, width=width, grouped=grouped),
        grid=(nseq, nt),
        in_specs=in_specs,
        out_specs=out_specs,
        out_shape=out_shape,
        scratch_shapes=[pltpu.VMEM((tm + ROW_GROUP, 3 * width), F32)],
        compiler_params=_params(("arbitrary", "arbitrary")),
        name="dn_in",
    )(*args)


def _delta_group(heads, rows, u, refs, consts, *, chunk, unit, nh, use_t):
    (q_ref, k_ref, v_ref, z_ref, og_ref, o_ref, s_scr, bg_ref, gcum_scr, grev_scr, gt_scr, mk_ref) = refs

    def spread(ref, lane):
        return jnp.broadcast_to(ref[rows, lane:lane + 1], (unit, HEAD_DIM))

    n_merge, nlev = consts
    hsl = [slice(h * HEAD_DIM, (h + 1) * HEAD_DIM) for h in heads]
    idx = range(len(heads))
    m_strict = mk_ref[0]
    m_incl = mk_ref[1]
    eye = mk_ref[3 + n_merge]

    k = [k_ref[rows, hs] for hs in hsl]
    q = [q_ref[rows, hs] for hs in hsl]
    b = [spread(bg_ref, h) for h in heads]
    gc = [spread(gcum_scr, nh + h) for h in heads]
    k_bf = [x.astype(BF16) for x in k]
    kb = [k[j] * b[j] for j in idx]
    kk = [_dot_nt(kb[j].astype(BF16), k_bf[j]) for j in idx]
    qkr = [_dot_nt(q[j].astype(BF16), k_bf[j]) for j in idx]
    if use_t:
        gi = [jnp.concatenate([gc[j]] * (unit // HEAD_DIM), axis=1) for j in idx]
        gj = [gt_scr[u, pl.ds(nh + h, 1), :] for h in heads]
    else:
        gi = [gc[j][:, :unit] for j in idx]
        ones = jnp.ones((unit, unit), BF16)
        gj = [_dot01_l(ones, gi[j] * eye) for j in idx]
    dec = [jnp.exp(jnp.minimum(gi[j] - gj[j], 0.0)) for j in idx]
    a = [(kk[j] * dec[j] * m_strict).astype(BF16) for j in idx]
    qk = [(qkr[j] * dec[j] * m_incl).astype(BF16) for j in idx]

    m_base = mk_ref[2].astype(BF16)
    d_bf = [a[j] * m_base for j in idx]
    p = [_dot(d_bf[j], d_bf[j]) for j in idx]
    r = [-d_bf[j].astype(F32) for j in idx]
    for lev in range(nlev):
        p_bf = [x.astype(BF16) for x in p]
        r = [r[j] + p[j] + _dot(r[j].astype(BF16), p_bf[j]) for j in idx]
        if lev + 1 < nlev:
            p = [_dot(p_bf[j], p_bf[j]) for j in idx]
    t_bf = [(eye + r[j]).astype(BF16) for j in idx]
    for lvl in range(n_merge):
        m_l = mk_ref[3 + lvl].astype(BF16)
        x = [_dot(t_bf[j], a[j] * m_l) for j in idx]
        t_bf = [(t_bf[j].astype(F32) - _dot(x[j].astype(BF16), t_bf[j])).astype(BF16) for j in idx]
    r = [t_bf[j].astype(F32) - eye for j in idx]

    eg = [jnp.exp(gc[j]) for j in idx]
    rhs = [jnp.concatenate([v_ref[rows, hsl[j]] * b[j], kb[j] * eg[j]], axis=1) for j in idx]
    uw = [rhs[j] + _dot(r[j].astype(BF16), rhs[j].astype(BF16)) for j in idx]
    qd = [q[j] * eg[j] for j in idx]
    kd = [k[j] * jnp.exp(spread(grev_scr, nh + h)) for j, h in enumerate(heads)]

    s = [s_scr[h] for h in heads]
    vn = [[] for _ in idx]
    oq = [[] for _ in idx]
    for c in range(unit // chunk):
        cs = slice(c * chunk, (c + 1) * chunk)
        wq = [_dot(jnp.concatenate([uw[j][cs, HEAD_DIM:], qd[j][cs]], axis=0).astype(BF16),
                   s[j].astype(BF16)) for j in idx]
        for j in idx:
            vn[j].append((uw[j][cs, :HEAD_DIM] - wq[j][:chunk]).astype(BF16))
            oq[j].append(wq[j][chunk:])
        last = (c + 1) * chunk - 1
        s = [s[j] * jnp.exp(gc[j][last:last + 1, :]) + _dot_tn(kd[j][cs].astype(BF16), vn[j][-1])
             for j in idx]
    for j, h in enumerate(heads):
        s_scr[h] = s[j]
    for j in idx:
        o = jnp.concatenate(oq[j], axis=0) + _dot(qk[j], jnp.concatenate(vn[j], axis=0))
        on = o * lax.rsqrt(jnp.mean(o * o, axis=-1, keepdims=True) + NORM_EPS) * og_ref[...]
        o_ref[rows, hsl[j]] = (on * _silu(z_ref[rows, hsl[j]])).astype(o_ref.dtype)


def _delta_kernel(q_ref, k_ref, v_ref, z_ref, bg_ref, s0_ref, og_ref,
                  ltri_ref, urev_ref, ltrit_ref, mk_ref,
                  o_ref, sout_ref, s_scr, gcum_scr, grev_scr, gt_scr,
                  *, tm, chunk, nh, unit, hgroup):
    i = pl.program_id(1)
    n_units = tm // unit
    base = min(INV_BASE, chunk)
    nlev = max(base.bit_length() - 2, 0)
    n_merge = mk_ref.shape[0] - 4
    use_t = unit % HEAD_DIM == 0

    @pl.when(i == 0)
    def _():
        s_scr[...] = s0_ref[0]

    bg = bg_ref[...]
    if use_t:
        bgt = bg.T
    for u in range(n_units):
        rs = slice(u * unit, (u + 1) * unit)
        g_hi, g_lo = _split2(bg[rs, :])
        gcum_scr[rs, :] = _dot(ltri_ref[...], g_hi) + _dot(ltri_ref[...], g_lo)
        grev_scr[rs, :] = _dot(urev_ref[...], g_hi) + _dot(urev_ref[...], g_lo)
        if use_t:
            gt_scr[u] = _dot01_r(bgt[:, rs], ltrit_ref[...])

    refs = (q_ref, k_ref, v_ref, z_ref, og_ref, o_ref, s_scr, bg_ref, gcum_scr, grev_scr, gt_scr, mk_ref)

    def unit_body(u, carry):
        rows = pl.ds(pl.multiple_of(u * unit, unit), unit)
        for g0 in range(0, nh, hgroup):
            _delta_group(list(range(g0, g0 + hgroup)), rows, u, refs, (n_merge, nlev),
                         chunk=chunk, unit=unit, nh=nh, use_t=use_t)
        return carry

    lax.fori_loop(0, n_units, unit_body, 0)

    @pl.when(i == pl.num_programs(1) - 1)
    def _():
        sout_ref[0] = s_scr[...]


def _delta(q, k, v, z, bg, s0, og, *, nseq, tm, chunk):
    t, width = q.shape
    nh = width // HEAD_DIM
    nt = t // (nseq * tm)
    unit = min(DN_UNIT, tm)
    assert tm % unit == 0 and unit % chunk == 0
    row = lambda b, i: (b * nt + i, 0)
    ri = jnp.arange(unit)[:, None]
    ci = jnp.arange(unit)[None, :]
    same = (ri // chunk) == (ci // chunk)
    ltri = jnp.logical_and(same, ri >= ci)
    urev = jnp.logical_and(same, ci > ri)
    base = min(INV_BASE, chunk)
    masks = [jnp.logical_and(same, ri > ci), ltri, (ri // base) == (ci // base)]
    m = base
    while m < chunk:
        masks.append(jnp.logical_and(ri // (2 * m) == ci // (2 * m), ri // m != ci // m))
        m *= 2
    masks.append(ri == ci)
    mk = jnp.stack(masks).astype(F32)
    ltri_b = ltri.astype(BF16)
    return pl.pallas_call(
        functools.partial(_delta_kernel, tm=tm, chunk=chunk, nh=nh, unit=unit, hgroup=nh),
        grid=(nseq, nt),
        in_specs=[pl.BlockSpec((tm, width), row)] * 4 + [
            pl.BlockSpec((tm, HEAD_DIM), row),
            pl.BlockSpec((1, nh, HEAD_DIM, HEAD_DIM), lambda b, i: (b, 0, 0, 0)),
            _const_spec((1, HEAD_DIM)),
            _const_spec((unit, unit)), _const_spec((unit, unit)), _const_spec((unit, unit)),
            _const_spec(mk.shape)],
        out_specs=[pl.BlockSpec((tm, width), row),
                   pl.BlockSpec((1, nh, HEAD_DIM, HEAD_DIM), lambda b, i: (b, 0, 0, 0))],
        out_shape=[jax.ShapeDtypeStruct((t, width), BF16 if tm % 16 == 0 else F32),
                   jax.ShapeDtypeStruct((nseq, nh, HEAD_DIM, HEAD_DIM), F32)],
        scratch_shapes=[pltpu.VMEM((nh, HEAD_DIM, HEAD_DIM), F32),
                        pltpu.VMEM((tm, HEAD_DIM), F32), pltpu.VMEM((tm, HEAD_DIM), F32),
                        pltpu.VMEM((tm // unit, HEAD_DIM, max(unit, HEAD_DIM)), F32)],
        compiler_params=_params(("arbitrary", "arbitrary")),
        name="delta",
    )(q, k, v, z, bg, s0, og, ltri_b, urev.astype(BF16), ltri_b.T, mk)


def _mixffn_kernel(mix_ref, x_ref, wo_ref, g1_ref, gt1_ref, g2_ref, sc2_ref, sh2_ref,
                   wi_ref, wout_ref, g3_ref, gt2_ref, y_ref, *, hidden, hchunk):
    a = _dot(mix_ref[...].astype(BF16), wo_ref[...])
    x1 = x_ref[...] + gt1_ref[0] * (a * _rms_scale(a) * g1_ref[...])
    h = (x1 * _rms_scale(x1) * g2_ref[...] * (1.0 + sc2_ref[0]) + sh2_ref[0]).astype(BF16)
    f = None
    for c0 in range(0, hidden, hchunk):
        gate = _dot(h, wi_ref[:, c0:c0 + hchunk])
        up = _dot(h, wi_ref[:, hidden + c0:hidden + c0 + hchunk])
        part = _dot((_silu(gate) * up).astype(BF16), wout_ref[c0:c0 + hchunk, :])
        f = part if f is None else f + part
    y_ref[...] = x1 + gt2_ref[0] * (f * _rms_scale(f) * g3_ref[...])


def _mixffn(mix, x, wo, g1, gt1, g2, sc2, sh2, wi, wout, g3, gt2, *, nseq, tm):
    t, d = x.shape
    nt = t // (nseq * tm)
    hidden = wout.shape[0]
    hchunk = hidden // 4
    row = lambda b, i: (b * nt + i, 0)
    vec = _const_spec((1, d))
    return pl.pallas_call(
        functools.partial(_mixffn_kernel, hidden=hidden, hchunk=hchunk),
        grid=(nseq, nt),
        in_specs=[pl.BlockSpec((tm, mix.shape[1]), row), pl.BlockSpec((tm, d), row),
                  _const_spec(wo.shape), vec, _mod_spec(gt1, tm), vec, _mod_spec(sc2, tm),
                  _mod_spec(sh2, tm), _const_spec(wi.shape), _const_spec(wout.shape), vec,
                  _mod_spec(gt2, tm)],
        out_specs=pl.BlockSpec((tm, d), row),
        out_shape=jax.ShapeDtypeStruct((t, d), F32),
        compiler_params=_params(("arbitrary", "arbitrary")),
        name="mixffn",
    )(mix, x, wo, g1, gt1, g2, sc2, sh2, wi, wout, g3, gt2)


def _attn_in_kernel(x_ref, gq_ref, scq_ref, shq_ref, gk_ref, sck_ref, shk_ref, wq_ref, wkv_ref,
                    q_ref, k_ref, v_ref, kb_ref, vb_ref, *, width):
    x = x_ref[...]
    xn = x * _rms_scale(x)
    hq = (xn * gq_ref[...] * (1.0 + scq_ref[0]) + shq_ref[0]).astype(BF16)
    hk = (xn * gk_ref[...] * (1.0 + sck_ref[0]) + shk_ref[0]).astype(BF16)
    q_ref[...] = (_dot(hq, wq_ref[...]) * (LOG2E * HEAD_DIM ** -0.5)).astype(BF16)
    k = _dot(hk, wkv_ref[:, :width])
    v = _dot(hk, wkv_ref[:, width:])
    k_ref[...] = k
    v_ref[...] = v
    kb_ref[...] = k.astype(BF16)
    vb_ref[...] = v.astype(BF16)


def _attn_in(x, gq, scq, shq, gk, sck, shk, wq, wkv, *, nseq, tm):
    t, d = x.shape
    width = wq.shape[1]
    nt = t // (nseq * tm)
    row = lambda b, i: (b * nt + i, 0)
    vec = _const_spec((1, d))
    blk = pl.BlockSpec((tm, width), row)
    return pl.pallas_call(
        functools.partial(_attn_in_kernel, width=width),
        grid=(nseq, nt),
        in_specs=[pl.BlockSpec((tm, d), row), vec, _mod_spec(scq, tm), _mod_spec(shq, tm),
                  vec, _mod_spec(sck, tm), _mod_spec(shk, tm),
                  _const_spec(wq.shape), _const_spec(wkv.shape)],
        out_specs=[blk] * 5,
        out_shape=[jax.ShapeDtypeStruct((t, width), BF16), jax.ShapeDtypeStruct((t, width), F32),
                   jax.ShapeDtypeStruct((t, width), F32), jax.ShapeDtypeStruct((t, width), BF16),
                   jax.ShapeDtypeStruct((t, width), BF16)],
        compiler_params=_params(("arbitrary", "arbitrary")),
        name="attn_in",
    )(x, gq, scq, shq, gk, sck, shk, wq, wkv)


def _causal_mask(shape):
    ri = lax.broadcasted_iota(jnp.int32, shape, 0)
    ci = lax.broadcasted_iota(jnp.int32, shape, 1)
    return ci < ri


def _sb_prompt_kernel(bias_ref, tri_ref, q_ref, k_ref, v_ref, o_ref, z_scr, acc_scr, carry_scr,
                      *, tq, hpb):
    hg = pl.program_id(1)
    i = pl.program_id(2)
    heads = range(hpb)
    hsl = [slice(j * HEAD_DIM, (j + 1) * HEAD_DIM) for j in heads]
    qs = [q_ref[:, hs] for hs in hsl]
    biases = [bias_ref[hg * hpb + j] * LOG2E for j in heads]

    def rows(b):
        return pl.ds(pl.multiple_of(jnp.maximum(i - b, 0) * tq, tq), tq)

    def logits(b, slot):
        r = rows(b)
        for j in heads:
            z_scr[slot, j] = _dot_nt(qs[j], k_ref[r, hsl[j]]) + biases[j]

    def consume(b, slot, first):
        r = rows(b)
        lsig, sp_bf = [], []
        for j in heads:
            z = z_scr[slot, j]
            sp = _softplus2(z)
            if first:
                mask = _causal_mask(z.shape)
                lsig.append(z - sp)
                sp = jnp.where(mask, sp, 0.0)
                carry_scr[j] = jnp.sum(sp, axis=-1, keepdims=True)
            else:
                carry = carry_scr[j]
                lsig.append(z - sp - carry)
                carry_scr[j] = carry + jnp.sum(sp, axis=-1, keepdims=True)
            sp_bf.append(sp.astype(BF16))
        att = [jnp.exp2(lsig[j] - _dot(sp_bf[j], tri_ref[...])) for j in heads]
        if first:
            att = [jnp.where(mask, att[j], 0.0) for j in heads]
        for j in heads:
            contrib = _dot(att[j].astype(BF16), v_ref[r, hsl[j]])
            acc_scr[j] = contrib if first else acc_scr[j] + contrib

    logits(0, 0)
    logits(1, 1)
    consume(0, 0, True)

    def pair(p, c):
        b = 2 * p + 1
        logits(b + 1, 0)
        consume(b, 1, False)
        logits(b + 2, 1)
        consume(b + 1, 0, False)
        return c

    lax.fori_loop(0, i // 2, pair, 0)

    @pl.when(i % 2 == 1)
    def _():
        consume(i, 1, False)

    for j in heads:
        o_ref[:, hsl[j]] = acc_scr[j].astype(o_ref.dtype)


def _sb_prompt(q, kb, vb, bias, *, nseq, tq, hpb):
    t, width = q.shape
    nh = width // HEAD_DIM
    seq = t // nseq
    nq = seq // tq
    ri = jnp.arange(tq)[:, None]
    ci = jnp.arange(tq)[None, :]
    tri = (ri > ci).astype(BF16)
    wblk = hpb * HEAD_DIM
    return pl.pallas_call(
        functools.partial(_sb_prompt_kernel, tq=tq, hpb=hpb),
        grid=(nseq, nh // hpb, nq),
        in_specs=[pl.BlockSpec(memory_space=pltpu.SMEM),
                  _const_spec((tq, tq)),
                  pl.BlockSpec((tq, wblk), lambda b, h, i: (b * nq + i, h)),
                  pl.BlockSpec((seq, wblk), lambda b, h, i: (b, h)),
                  pl.BlockSpec((seq, wblk), lambda b, h, i: (b, h))],
        out_specs=pl.BlockSpec((tq, wblk), lambda b, h, i: (b * nq + i, h)),
        out_shape=jax.ShapeDtypeStruct((t, width), BF16),
        scratch_shapes=[pltpu.VMEM((2, hpb, tq, tq), F32), pltpu.VMEM((hpb, tq, HEAD_DIM), F32),
                        pltpu.VMEM((hpb, tq, 1), F32)],
        compiler_params=_params(("arbitrary", "arbitrary", "arbitrary")),
        name="sb_prompt",
    )(bias, tri, q, kb, vb)


def _dec_blocks(kcats, vcats, qbd, bias, carry, keymask):
    nb = range(len(kcats))
    z = [_dot_nt(qbd, kcats[p]) + bias for p in nb]
    sp = [_softplus2(z[p]) for p in nb]
    if keymask is not None:
        sp = [jnp.where(keymask, sp[p], 0.0) for p in nb]
    n = z[0].shape[1]
    ri = lax.broadcasted_iota(jnp.int32, (n, n), 0)
    ci = lax.broadcasted_iota(jnp.int32, (n, n), 1)
    mrev = (ri > ci).astype(BF16)
    within = [_dot(sp[p].astype(BF16), mrev) for p in nb]
    out = None
    for p in nb:
        att = jnp.exp2(z[p] - sp[p] - within[p] - carry)
        if keymask is not None:
            att = jnp.where(keymask, att, 0.0)
        contrib = _dot(att.astype(BF16), vcats[p])
        out = contrib if out is None else out + contrib
        carry = carry + jnp.sum(sp[p], axis=1, keepdims=True)
    return out, carry


def _sb_decode_kernel(pt_ref, qbd_ref, bias_ref, knew_ref, vnew_ref, *rest, pages_per_step, nh):
    del pt_ref
    pp = pages_per_step
    k_refs = rest[:pp]
    v_refs = rest[pp:2 * pp]
    o_ref, acc_ref, carry_ref = rest[2 * pp:]
    j = pl.program_id(1)
    qbd = qbd_ref[0]
    bias = bias_ref[...] * LOG2E
    w = qbd.shape[0]

    @pl.when(j == 0)
    def _():
        s_idx = lax.broadcasted_iota(jnp.int32, (w, ROW_GROUP), 1)
        i_idx = lax.broadcasted_iota(jnp.int32, (w, ROW_GROUP), 0) % ROW_GROUP
        real = jnp.logical_and(s_idx >= HIST_ROWS, s_idx < ROW_GROUP - 1)
        keymask = jnp.logical_and(real, s_idx < i_idx)
        contrib, carry = _dec_blocks([knew_ref[0].astype(BF16)], [vnew_ref[0].astype(BF16)], qbd, bias,
                                     jnp.zeros((w, 1), F32), keymask)
        acc_ref[...] = contrib
        carry_ref[...] = jnp.broadcast_to(carry, carry_ref.shape)

    n_tok = k_refs[0].shape[0] // nh

    def cat(ref):
        return jnp.concatenate([ref[pl.ds(h, n_tok, stride=nh), :].astype(BF16) for h in range(nh)], axis=1)

    contrib, carry = _dec_blocks([cat(r) for r in k_refs], [cat(r) for r in v_refs], qbd, bias,
                                 carry_ref[:, 0:1], None)
    acc = acc_ref[...] + contrib
    acc_ref[...] = acc
    carry_ref[...] = jnp.broadcast_to(carry, carry_ref.shape)

    @pl.when(j == pl.num_programs(1) - 1)
    def _():
        o_ref[0] = jnp.concatenate(
            [acc[h * ROW_GROUP:(h + 1) * ROW_GROUP, h * HEAD_DIM:(h + 1) * HEAD_DIM] for h in range(nh)],
            axis=0).astype(o_ref.dtype)


def _sb_decode(qbd, bias, knew, vnew, cache_k2, cache_v2, page_table, *, n_pool, pages_per_step):
    nb, n_pages = page_table.shape
    pp = pages_per_step
    steps = n_pages // pp
    w, width = qbd.shape[1:]
    nh = width // HEAD_DIM
    page_rows = cache_k2.shape[0] // n_pool

    def page_spec(r):
        def imap(b, j, pt):
            return (pt[b * n_pages + (n_pages - 1 - (j * pp + r))], 0)
        return pl.BlockSpec((page_rows, HEAD_DIM), imap)

    seq3 = lambda b, j, pt: (b, 0, 0)
    grid_spec = pltpu.PrefetchScalarGridSpec(
        num_scalar_prefetch=1,
        grid=(nb, steps),
        in_specs=[pl.BlockSpec((1, w, width), seq3),
                  pl.BlockSpec((w, 1), lambda b, j, pt: (0, 0)),
                  pl.BlockSpec((1, ROW_GROUP, width), seq3),
                  pl.BlockSpec((1, ROW_GROUP, width), seq3)]
        + [page_spec(r) for r in range(pp)] * 2,
        out_specs=pl.BlockSpec((1, w, HEAD_DIM), seq3),
        scratch_shapes=[pltpu.VMEM((w, width), F32), pltpu.VMEM((w, HEAD_DIM), F32)],
    )
    return pl.pallas_call(
        functools.partial(_sb_decode_kernel, pages_per_step=pp, nh=nh),
        grid_spec=grid_spec,
        out_shape=jax.ShapeDtypeStruct((nb, w, HEAD_DIM), BF16),
        compiler_params=_params(("arbitrary", "arbitrary")),
        name="sb_decode",
    )(page_table.reshape(-1), qbd, bias, knew, vnew,
      *([cache_k2] * pp), *([cache_v2] * pp))


def _mods(m, n, rows):
    parts = jnp.split(m, n, axis=-1)
    if rows is None:
        return [p[:, None, :] for p in parts]
    return [jnp.repeat(p, rows, axis=0)[None] for p in parts]


def _trunk(x2d, m0, m1, mkv, wts, *, nseq, tm, ffn_tm, rows_per_mod, dn_nseq, dn_tm, dn_chunk, s0, inj,
           attend):
    sh1, sc1, gt1, sh2, sc2, gt2 = _mods(m0, 6, rows_per_mod)
    dn = _dn_in(x2d, sc1, sh1, wts["mix_pre_g"][0], wts["dn_w_qkvz"], wts["dn_w_bg"],
                wts["dn_conv_w"], wts["dn_ap"], inj, nseq=nseq, tm=tm)
    q, k, v, z, bg, hist = dn[:6]
    xx = dn[6] if inj is not None else None
    o_dn, s_new = _delta(q, k, v, z, bg, s0, wts["dn_onorm_g"], nseq=dn_nseq, tm=dn_tm,
                         chunk=dn_chunk)
    x2 = _mixffn(o_dn, x2d, wts["dn_w_out"], wts["mix_post_g"][0], gt1, wts["ffn_pre_g"][0], sc2, sh2,
                 wts["ffn_w_in"][0], wts["ffn_w_out"][0], wts["ffn_post_g"][0], gt2, nseq=nseq, tm=ffn_tm)

    sh1, sc1, gt1, sh2, sc2, gt2 = _mods(m1, 6, rows_per_mod)
    ksh, ksc = _mods(mkv, 2, rows_per_mod)
    qb, k_new, v_new, kb, vb = _attn_in(x2, wts["mix_pre_g"][1], sc1, sh1, wts["kv_norm_g"], ksc, ksh,
                                        wts["sb_w_q"], wts["sb_w_kv"], nseq=nseq, tm=tm)
    o_sb = attend(qb, k_new, v_new, kb, vb)
    y = _mixffn(o_sb, x2, wts["sb_w_o"], wts["mix_post_g"][1], gt1, wts["ffn_pre_g"][1], sc2, sh2,
                wts["ffn_w_in"][1], wts["ffn_w_out"][1], wts["ffn_post_g"][1], gt2, nseq=nseq, tm=ffn_tm)
    return y, hist, xx, s_new, k_new, v_new


def kernel(x_prompt, x_sample, c_prompt, c_sample, state_dn_conv, state_dn_S, cache_k, cache_v, page_table, ada_w, ada_b, mix_pre_g, mix_post_g, ffn_pre_g, ffn_post_g, ffn_w_in, ffn_w_out, dn_w_in, dn_conv_w, dn_a_log, dn_dt_bias, dn_onorm_g, dn_w_out, kv_ada_w, kv_ada_b, kv_norm_g, sb_w_kv, sb_w_q, sb_logit_bias, sb_w_o):
    bp, seq, d = x_prompt.shape
    db, dseq, _ = x_sample.shape
    nh = d // HEAD_DIM
    width = nh * HEAD_DIM
    assert ada_w.shape[0] == 2 and dn_w_in.shape[0] == 1 and sb_w_q.shape[0] == 1
    assert dseq == ROW_GROUP - HIST_ROWS - 1 and cache_k.shape[1] == HEAD_DIM and cache_k.shape[2] == nh

    vec = lambda g: g.reshape(g.shape[0], 1, g.shape[1])
    w_in = dn_w_in[0]
    pad_lanes = HEAD_DIM - 2 * nh
    zpad = jnp.zeros((nh,), F32)
    lane_pad = lambda a: jnp.concatenate([zpad, a, jnp.zeros((pad_lanes,), F32)])
    wts = {
        "mix_pre_g": vec(mix_pre_g), "mix_post_g": vec(mix_post_g),
        "ffn_pre_g": vec(ffn_pre_g), "ffn_post_g": vec(ffn_post_g),
        "ffn_w_in": [w.astype(BF16) for w in ffn_w_in], "ffn_w_out": [w.astype(BF16) for w in ffn_w_out],
        "dn_w_qkvz": w_in[:, :4 * width].astype(BF16),
        "dn_w_bg": jnp.pad(w_in[:, 4 * width:], ((0, 0), (0, pad_lanes))).astype(BF16),
        "dn_conv_w": dn_conv_w[0],
        "dn_ap": jnp.stack([lane_pad(-jnp.exp(dn_a_log[0])), lane_pad(dn_dt_bias[0])]),
        "dn_onorm_g": dn_onorm_g,
        "dn_w_out": dn_w_out[0].astype(BF16),
        "kv_norm_g": kv_norm_g.reshape(1, d),
        "sb_w_kv": sb_w_kv.astype(BF16), "sb_w_q": sb_w_q[0].astype(BF16),
        "sb_w_o": sb_w_o[0].astype(BF16),
    }

    c_all = jnp.concatenate([c_prompt, c_sample], axis=0)
    c_all = jnp.pad(c_all, ((0, -(bp + db) % 16), (0, 0)))
    m0 = _ada(c_all, ada_w[0], ada_b[0])
    m1 = _ada(c_all, ada_w[1], ada_b[1])
    mkv = _ada(c_all, kv_ada_w, kv_ada_b)

    tm_p = min(ROW_TILE, seq)
    chunk_p = min(DN_CHUNK, seq)
    dn_tm_p = min(DN_ROW_TILE, seq)
    tq = min(SB_BLOCK, seq)
    hpb = min(SB_HEADS_PER_STEP, nh)
    bias_p = sb_logit_bias[0]

    def attend_prompt(qb, k_new, v_new, kb, vb):
        return _sb_prompt(qb, kb, vb, bias_p, nseq=bp, tq=tq, hpb=hpb)

    y_p, hist_p, _, s_p, k_p, v_p = _trunk(
        x_prompt.reshape(bp * seq, d), m0[:bp], m1[:bp], mkv[:bp], wts,
        nseq=bp, tm=tm_p, ffn_tm=min(FFN_ROW_TILE, seq), rows_per_mod=None, dn_nseq=bp, dn_tm=dn_tm_p, dn_chunk=chunk_p,
        s0=jnp.zeros((bp, nh, HEAD_DIM, HEAD_DIM), F32), inj=None, attend=attend_prompt)

    rows_s = db * ROW_GROUP
    pad_t = ROW_GROUP - HIST_ROWS - dseq
    xs = jnp.pad(x_sample, ((0, 0), (HIST_ROWS, pad_t), (0, 0))).reshape(rows_s, d)
    inj = jnp.pad(state_dn_conv[0], ((0, 0), (0, ROW_GROUP - HIST_ROWS), (0, 0))).reshape(rows_s, -1)
    n_pool = cache_k.shape[0]
    ck2 = cache_k.reshape(n_pool * HEAD_DIM * nh, HEAD_DIM)
    cv2 = cache_v.reshape(n_pool * HEAD_DIM * nh, HEAD_DIM)
    bias_s = jnp.repeat(sb_logit_bias[0], ROW_GROUP)[:, None]
    n_pages = page_table.shape[1]
    pps = DECODE_PAGES_PER_STEP if n_pages % DECODE_PAGES_PER_STEP == 0 else 1

    def attend_sample(qb, k_new, v_new, kb, vb):
        qt = qb.reshape(db, ROW_GROUP, nh, HEAD_DIM).transpose(0, 2, 1, 3)
        qbd = (qt[:, :, :, None, :] * jnp.eye(nh, dtype=BF16)[None, :, None, :, None]).reshape(
            db, nh * ROW_GROUP, width)
        knew = k_new.reshape(db, ROW_GROUP, width)
        vnew = v_new.reshape(db, ROW_GROUP, width)
        o = _sb_decode(qbd, bias_s, knew, vnew, ck2, cv2, page_table, n_pool=n_pool, pages_per_step=pps)
        return o.reshape(db, nh, ROW_GROUP, HEAD_DIM).transpose(0, 2, 1, 3).reshape(rows_s, width)

    tm_s = min(ROW_TILE, rows_s)
    y_s, _, xx_s, s_s, k_s, v_s = _trunk(
        xs, m0[bp:bp + db], m1[bp:bp + db], mkv[bp:bp + db], wts,
        nseq=1, tm=tm_s, ffn_tm=tm_s, rows_per_mod=ROW_GROUP, dn_nseq=db, dn_tm=ROW_GROUP, dn_chunk=ROW_GROUP,
        s0=state_dn_S[0], inj=inj, attend=attend_sample)

    real = slice(HIST_ROWS, HIST_ROWS + dseq)
    grp = lambda a: a.reshape((db, ROW_GROUP) + a.shape[1:])
    return (
        y_p.reshape(bp, seq, d),
        grp(y_s)[:, real],
        hist_p[None, :, ROW_GROUP - HIST_ROWS:],
        s_p[None],
        k_p.reshape(bp, seq, nh, HEAD_DIM),
        v_p.reshape(bp, seq, nh, HEAD_DIM),
        grp(xx_s)[None, :, dseq:dseq + HIST_ROWS],
        s_s[None],
        grp(k_s)[:, real].reshape(db, dseq, nh, HEAD_DIM),
        grp(v_s)[:, real].reshape(db, dseq, nh, HEAD_DIM),
    )
```

```python
import functools
import math

import jax
import jax.numpy as jnp
from jax import lax
from jax.experimental import pallas as pl
from jax.experimental.pallas import tpu as pltpu

F32 = jnp.float32
BF16 = jnp.bfloat16

HEAD_DIM = 128
NORM_EPS = 1e-6
L2_EPS = 1e-6
DN_CONV = 4
HIST_ROWS = DN_CONV - 1
ROW_GROUP = 8
DN_CHUNK = 64
DN_UNIT = 256
INV_BASE = 8
LOG2E = math.log2(math.e)
EXP2_CLAMP = 100.0
V7X_VMEM_LIMIT = 56 * 1024 * 1024

ROW_TILE = 256
FFN_ROW_TILE = 512
DN_ROW_TILE = 512
SB_BLOCK = 256
SB_HEADS_PER_STEP = 4
DECODE_PAGES_PER_STEP = 16


def _params(sem, vmem=V7X_VMEM_LIMIT):
    return pltpu.CompilerParams(dimension_semantics=sem, vmem_limit_bytes=vmem)


def _dot(a, b):
    return jnp.dot(a, b, preferred_element_type=F32)


def _dot_nt(a, b):
    return lax.dot_general(a, b, (((1,), (1,)), ((), ())), preferred_element_type=F32)


def _dot_tn(a, b):
    return lax.dot_general(a, b, (((0,), (0,)), ((), ())), preferred_element_type=F32)


def _split2(x):
    hi = x.astype(BF16)
    lo = (x - hi.astype(F32)).astype(BF16)
    return hi, lo


def _dot01_l(m01, x):
    hi, lo = _split2(x)
    return _dot(m01, hi) + _dot(m01, lo)


def _dot01_r(x, m01):
    hi, lo = _split2(x)
    return _dot(hi, m01) + _dot(lo, m01)


def _sigmoid(x):
    return 1.0 / (1.0 + jnp.exp(-x))


def _silu(x):
    return x * _sigmoid(x)


def _softplus(x):
    return jnp.maximum(x, 0.0) + jnp.log(1.0 + jnp.exp(-jnp.abs(x)))


def _softplus2(z):
    e = jnp.exp2(jnp.minimum(z, EXP2_CLAMP))
    return jnp.maximum(jnp.log(1.0 + e) * LOG2E, z)


def _rms_scale(x):
    return lax.rsqrt(jnp.mean(x * x, axis=-1, keepdims=True) + NORM_EPS)


def _const_spec(shape):
    nd = len(shape)
    return pl.BlockSpec(shape, lambda *_: (0,) * nd, pipeline_mode=pl.Buffered(1))


def _mod_spec(arr, tm):
    if arr.shape[1] == 1:
        return pl.BlockSpec((1, 1, arr.shape[2]), lambda b, i: (b, 0, 0))
    return pl.BlockSpec((1, tm, arr.shape[2]), lambda b, i: (0, i, 0))


def _ada_kernel(c_ref, w_ref, b_ref, o_ref):
    c = _silu(c_ref[...])
    c_hi, c_lo = _split2(c)
    w_hi, w_lo = _split2(w_ref[...])
    o_ref[...] = _dot(c_hi, w_hi) + _dot(c_lo, w_hi) + _dot(c_hi, w_lo) + b_ref[...]


def _ada(c, w, b):
    r, d = c.shape
    n = w.shape[1]
    tn = 1024
    return pl.pallas_call(
        _ada_kernel,
        grid=(n // tn,),
        in_specs=[pl.BlockSpec((r, d), lambda j: (0, 0)),
                  pl.BlockSpec((d, tn), lambda j: (0, j)),
                  pl.BlockSpec((1, tn), lambda j: (0, j))],
        out_specs=pl.BlockSpec((r, tn), lambda j: (0, j)),
        out_shape=jax.ShapeDtypeStruct((r, n), F32),
        compiler_params=_params(("arbitrary",)),
        name="ada",
    )(c, w, b.reshape(1, n))


def _dn_in_kernel(x_ref, sc_ref, sh_ref, g_ref, w_ref, wbg_ref, cw_ref, ap_ref, *rest,
                  tm, width, grouped):
    if grouped:
        inj_ref, q_ref, k_ref, v_ref, z_ref, bg_ref, hist_ref, xxo_ref, xx_ref = rest
    else:
        q_ref, k_ref, v_ref, z_ref, bg_ref, hist_ref, xx_ref = rest
    i = pl.program_id(1)
    x = x_ref[...]
    h = (x * _rms_scale(x) * g_ref[...] * (1.0 + sc_ref[0]) + sh_ref[0]).astype(BF16)

    @pl.when(i == 0)
    def _():
        xx_ref[0:ROW_GROUP, :] = jnp.zeros((ROW_GROUP, 3 * width), F32)

    for c in range(3):
        cs = slice(c * width, (c + 1) * width)
        p = _dot(h, w_ref[:, cs])
        if grouped:
            rr = lax.broadcasted_iota(jnp.int32, (tm, 1), 0) % ROW_GROUP
            m = (rr < HIST_ROWS).astype(F32)
            p = inj_ref[:, cs] * m + p * (1.0 - m)
            xxo_ref[:, cs] = p
        xx_ref[ROW_GROUP:ROW_GROUP + tm, cs] = p
    z_ref[...] = _dot(h, w_ref[:, 3 * width:4 * width])

    nh = width // HEAD_DIM
    bgl = _dot(h, wbg_ref[...])
    lane = lax.broadcasted_iota(jnp.int32, bgl.shape, 1)
    beta = _sigmoid(bgl)
    gdec = ap_ref[0:1, :] * _softplus(bgl + ap_ref[1:2, :])
    bg = jnp.where(lane < nh, beta, jnp.where(lane < 2 * nh, gdec, 0.0))
    if grouped:
        rr = lax.broadcasted_iota(jnp.int32, (tm, 1), 0) % ROW_GROUP
        active = jnp.logical_and(rr >= HIST_ROWS, rr < ROW_GROUP - 1).astype(F32)
        bg = bg * active
    bg_ref[...] = bg

    outs = (q_ref, k_ref, v_ref)
    for c in range(3):
        for hd in range(nh):
            cs = slice(c * width + hd * HEAD_DIM, c * width + (hd + 1) * HEAD_DIM)
            full = xx_ref[0:ROW_GROUP + tm, cs]
            y = full[ROW_GROUP:] * cw_ref[DN_CONV - 1:DN_CONV, cs]
            for back in range(1, DN_CONV):
                shifted = pltpu.roll(full, back, axis=0)[ROW_GROUP:]
                y = y + shifted * cw_ref[DN_CONV - 1 - back:DN_CONV - back, cs]
            y = _silu(y)
            if c < 2:
                norm = lax.rsqrt(jnp.sum(y * y, axis=-1, keepdims=True) + L2_EPS)
                y = y * (norm * (HEAD_DIM ** -0.5) if c == 0 else norm)
            outs[c][:, hd * HEAD_DIM:(hd + 1) * HEAD_DIM] = y

    tail = xx_ref[tm:tm + ROW_GROUP, :]
    hist_ref[0] = tail
    xx_ref[0:ROW_GROUP, :] = tail


def _dn_in(x, sc, sh, g, w, wbg, cw, ap, inj, *, nseq, tm):
    t, d = x.shape
    width = w.shape[1] // 4
    nt = t // (nseq * tm)
    grouped = inj is not None
    row = lambda b, i: (b * nt + i, 0)
    in_specs = [pl.BlockSpec((tm, d), row), _mod_spec(sc, tm), _mod_spec(sh, tm),
                _const_spec((1, d)), _const_spec(w.shape), _const_spec(wbg.shape),
                _const_spec(cw.shape), _const_spec(ap.shape)]
    args = [x, sc, sh, g, w, wbg, cw, ap]
    out_shape = [jax.ShapeDtypeStruct((t, width), F32)] * 4 + [
        jax.ShapeDtypeStruct((t, HEAD_DIM), F32),
        jax.ShapeDtypeStruct((nseq, ROW_GROUP, 3 * width), F32)]
    out_specs = [pl.BlockSpec((tm, width), row)] * 4 + [
        pl.BlockSpec((tm, HEAD_DIM), row),
        pl.BlockSpec((1, ROW_GROUP, 3 * width), lambda b, i: (b, 0, 0))]
    if grouped:
        in_specs.append(pl.BlockSpec((tm, 3 * width), row))
        args.append(inj)
        out_shape.append(jax.ShapeDtypeStruct((t, 3 * width), F32))
        out_specs.append(pl.BlockSpec((tm, 3 * width), row))
    return pl.pallas_call(
        functools.partial(_dn_in_kernel, tm=tm, width=width, grouped=grouped),
        grid=(nseq, nt),
        in_specs=in_specs,
        out_specs=out_specs,
        out_shape=out_shape,
        scratch_shapes=[pltpu.VMEM((tm + ROW_GROUP, 3 * width), F32)],
        compiler_params=_params(("arbitrary", "arbitrary")),
        name="dn_in",
    )(*args)


def _delta_group(heads, rows, u, refs, consts, *, chunk, unit, nh, use_t):
    (q_ref, k_ref, v_ref, z_ref, og_ref, o_ref, s_scr, bg_ref, gcum_scr, grev_scr, gt_scr, mk_ref) = refs

    def spread(ref, lane):
        return jnp.broadcast_to(ref[rows, lane:lane + 1], (unit, HEAD_DIM))

    n_merge, nlev = consts
    hsl = [slice(h * HEAD_DIM, (h + 1) * HEAD_DIM) for h in heads]
    idx = range(len(heads))
    m_strict = mk_ref[0]
    m_incl = mk_ref[1]
    eye = mk_ref[3 + n_merge]

    k = [k_ref[rows, hs] for hs in hsl]
    q = [q_ref[rows, hs] for hs in hsl]
    b = [spread(bg_ref, h) for h in heads]
    gc = [spread(gcum_scr, nh + h) for h in heads]
    k_bf = [x.astype(BF16) for x in k]
    kb = [k[j] * b[j] for j in idx]
    kk = [_dot_nt(kb[j].astype(BF16), k_bf[j]) for j in idx]
    qkr = [_dot_nt(q[j].astype(BF16), k_bf[j]) for j in idx]
    if use_t:
        gi = [jnp.concatenate([gc[j]] * (unit // HEAD_DIM), axis=1) for j in idx]
        gj = [gt_scr[u, pl.ds(nh + h, 1), :] for h in heads]
    else:
        gi = [gc[j][:, :unit] for j in idx]
        ones = jnp.ones((unit, unit), BF16)
        gj = [_dot01_l(ones, gi[j] * eye) for j in idx]
    dec = [jnp.exp(jnp.minimum(gi[j] - gj[j], 0.0)) for j in idx]
    a = [(kk[j] * dec[j] * m_strict).astype(BF16) for j in idx]
    qk = [(qkr[j] * dec[j] * m_incl).astype(BF16) for j in idx]

    m_base = mk_ref[2].astype(BF16)
    d_bf = [a[j] * m_base for j in idx]
    p = [_dot(d_bf[j], d_bf[j]) for j in idx]
    r = [-d_bf[j].astype(F32) for j in idx]
    for lev in range(nlev):
        p_bf = [x.astype(BF16) for x in p]
        r = [r[j] + p[j] + _dot(r[j].astype(BF16), p_bf[j]) for j in idx]
        if lev + 1 < nlev:
            p = [_dot(p_bf[j], p_bf[j]) for j in idx]
    t_bf = [(eye + r[j]).astype(BF16) for j in idx]
    for lvl in range(n_merge):
        m_l = mk_ref[3 + lvl].astype(BF16)
        x = [_dot(t_bf[j], a[j] * m_l) for j in idx]
        t_bf = [(t_bf[j].astype(F32) - _dot(x[j].astype(BF16), t_bf[j])).astype(BF16) for j in idx]
    r = [t_bf[j].astype(F32) - eye for j in idx]

    eg = [jnp.exp(gc[j]) for j in idx]
    rhs = [jnp.concatenate([v_ref[rows, hsl[j]] * b[j], kb[j] * eg[j]], axis=1) for j in idx]
    uw = [rhs[j] + _dot(r[j].astype(BF16), rhs[j].astype(BF16)) for j in idx]
    qd = [q[j] * eg[j] for j in idx]
    kd = [k[j] * jnp.exp(spread(grev_scr, nh + h)) for j, h in enumerate(heads)]

    s = [s_scr[h] for h in heads]
    vn = [[] for _ in idx]
    oq = [[] for _ in idx]
    for c in range(unit // chunk):
        cs = slice(c * chunk, (c + 1) * chunk)
        wq = [_dot(jnp.concatenate([uw[j][cs, HEAD_DIM:], qd[j][cs]], axis=0).astype(BF16),
                   s[j].astype(BF16)) for j in idx]
        for j in idx:
            vn[j].append((uw[j][cs, :HEAD_DIM] - wq[j][:chunk]).astype(BF16))
            oq[j].append(wq[j][chunk:])
        last = (c + 1) * chunk - 1
        s = [s[j] * jnp.exp(gc[j][last:last + 1, :]) + _dot_tn(kd[j][cs].astype(BF16), vn[j][-1])
             for j in idx]
    for j, h in enumerate(heads):
        s_scr[h] = s[j]
    for j in idx:
        o = jnp.concatenate(oq[j], axis=0) + _dot(qk[j], jnp.concatenate(vn[j], axis=0))
        on = o * lax.rsqrt(jnp.mean(o * o, axis=-1, keepdims=True) + NORM_EPS) * og_ref[...]
        o_ref[rows, hsl[j]] = (on * _silu(z_ref[rows, hsl[j]])).astype(o_ref.dtype)


def _delta_kernel(q_ref, k_ref, v_ref, z_ref, bg_ref, s0_ref, og_ref,
                  ltri_ref, urev_ref, ltrit_ref, mk_ref,
                  o_ref, sout_ref, s_scr, gcum_scr, grev_scr, gt_scr,
                  *, tm, chunk, nh, unit, hgroup):
    i = pl.program_id(1)
    n_units = tm // unit
    base = min(INV_BASE, chunk)
    nlev = max(base.bit_length() - 2, 0)
    n_merge = mk_ref.shape[0] - 4
    use_t = unit % HEAD_DIM == 0

    @pl.when(i == 0)
    def _():
        s_scr[...] = s0_ref[0]

    bg = bg_ref[...]
    if use_t:
        bgt = bg.T
    for u in range(n_units):
        rs = slice(u * unit, (u + 1) * unit)
        g_hi, g_lo = _split2(bg[rs, :])
        gcum_scr[rs, :] = _dot(ltri_ref[...], g_hi) + _dot(ltri_ref[...], g_lo)
        grev_scr[rs, :] = _dot(urev_ref[...], g_hi) + _dot(urev_ref[...], g_lo)
        if use_t:
            gt_scr[u] = _dot01_r(bgt[:, rs], ltrit_ref[...])

    refs = (q_ref, k_ref, v_ref, z_ref, og_ref, o_ref, s_scr, bg_ref, gcum_scr, grev_scr, gt_scr, mk_ref)

    def unit_body(u, carry):
        rows = pl.ds(pl.multiple_of(u * unit, unit), unit)
        for g0 in range(0, nh, hgroup):
            _delta_group(list(range(g0, g0 + hgroup)), rows, u, refs, (n_merge, nlev),
                         chunk=chunk, unit=unit, nh=nh, use_t=use_t)
        return carry

    lax.fori_loop(0, n_units, unit_body, 0)

    @pl.when(i == pl.num_programs(1) - 1)
    def _():
        sout_ref[0] = s_scr[...]


def _delta(q, k, v, z, bg, s0, og, *, nseq, tm, chunk):
    t, width = q.shape
    nh = width // HEAD_DIM
    nt = t // (nseq * tm)
    unit = min(DN_UNIT, tm)
    assert tm % unit == 0 and unit % chunk == 0
    row = lambda b, i: (b * nt + i, 0)
    ri = jnp.arange(unit)[:, None]
    ci = jnp.arange(unit)[None, :]
    same = (ri // chunk) == (ci // chunk)
    ltri = jnp.logical_and(same, ri >= ci)
    urev = jnp.logical_and(same, ci > ri)
    base = min(INV_BASE, chunk)
    masks = [jnp.logical_and(same, ri > ci), ltri, (ri // base) == (ci // base)]
    m = base
    while m < chunk:
        masks.append(jnp.logical_and(ri // (2 * m) == ci // (2 * m), ri // m != ci // m))
        m *= 2
    masks.append(ri == ci)
    mk = jnp.stack(masks).astype(F32)
    ltri_b = ltri.astype(BF16)
    return pl.pallas_call(
        functools.partial(_delta_kernel, tm=tm, chunk=chunk, nh=nh, unit=unit, hgroup=nh),
        grid=(nseq, nt),
        in_specs=[pl.BlockSpec((tm, width), row)] * 4 + [
            pl.BlockSpec((tm, HEAD_DIM), row),
            pl.BlockSpec((1, nh, HEAD_DIM, HEAD_DIM), lambda b, i: (b, 0, 0, 0)),
            _const_spec((1, HEAD_DIM)),
            _const_spec((unit, unit)), _const_spec((unit, unit)), _const_spec((unit, unit)),
            _const_spec(mk.shape)],
        out_specs=[pl.BlockSpec((tm, width), row),
                   pl.BlockSpec((1, nh, HEAD_DIM, HEAD_DIM), lambda b, i: (b, 0, 0, 0))],
        out_shape=[jax.ShapeDtypeStruct((t, width), BF16 if tm % 16 == 0 else F32),
                   jax.ShapeDtypeStruct((nseq, nh, HEAD_DIM, HEAD_DIM), F32)],
        scratch_shapes=[pltpu.VMEM((nh, HEAD_DIM, HEAD_DIM), F32),
                        pltpu.VMEM((tm, HEAD_DIM), F32), pltpu.VMEM((tm, HEAD_DIM), F32),
                        pltpu.VMEM((tm // unit, HEAD_DIM, max(unit, HEAD_DIM)), F32)],
        compiler_params=_params(("arbitrary", "arbitrary")),
        name="delta",
    )(q, k, v, z, bg, s0, og, ltri_b, urev.astype(BF16), ltri_b.T, mk)


def _mixffn_kernel(mix_ref, x_ref, wo_ref, g1_ref, gt1_ref, g2_ref, sc2_ref, sh2_ref,
                   wi_ref, wout_ref, g3_ref, gt2_ref, y_ref, *, hidden, hchunk):
    a = _dot(mix_ref[...].astype(BF16), wo_ref[...])
    x1 = x_ref[...] + gt1_ref[0] * (a * _rms_scale(a) * g1_ref[...])
    h = (x1 * _rms_scale(x1) * g2_ref[...] * (1.0 + sc2_ref[0]) + sh2_ref[0]).astype(BF16)
    f = None
    for c0 in range(0, hidden, hchunk):
        gate = _dot(h, wi_ref[:, c0:c0 + hchunk])
        up = _dot(h, wi_ref[:, hidden + c0:hidden + c0 + hchunk])
        part = _dot((_silu(gate) * up).astype(BF16), wout_ref[c0:c0 + hchunk, :])
        f = part if f is None else f + part
    y_ref[...] = x1 + gt2_ref[0] * (f * _rms_scale(f) * g3_ref[...])


def _mixffn(mix, x, wo, g1, gt1, g2, sc2, sh2, wi, wout, g3, gt2, *, nseq, tm):
    t, d = x.shape
    nt = t // (nseq * tm)
    hidden = wout.shape[0]
    hchunk = hidden
    row = lambda b, i: (b * nt + i, 0)
    vec = _const_spec((1, d))
    return pl.pallas_call(
        functools.partial(_mixffn_kernel, hidden=hidden, hchunk=hchunk),
        grid=(nseq, nt),
        in_specs=[pl.BlockSpec((tm, mix.shape[1]), row), pl.BlockSpec((tm, d), row),
                  _const_spec(wo.shape), vec, _mod_spec(gt1, tm), vec, _mod_spec(sc2, tm),
                  _mod_spec(sh2, tm), _const_spec(wi.shape), _const_spec(wout.shape), vec,
                  _mod_spec(gt2, tm)],
        out_specs=pl.BlockSpec((tm, d), row),
        out_shape=jax.ShapeDtypeStruct((t, d), F32),
        compiler_params=_params(("arbitrary", "arbitrary")),
        name="mixffn",
    )(mix, x, wo, g1, gt1, g2, sc2, sh2, wi, wout, g3, gt2)


def _attn_in_kernel(x_ref, gq_ref, scq_ref, shq_ref, gk_ref, sck_ref, shk_ref, wq_ref, wkv_ref,
                    q_ref, k_ref, v_ref, kb_ref, vb_ref, *, width):
    x = x_ref[...]
    xn = x * _rms_scale(x)
    hq = (xn * gq_ref[...] * (1.0 + scq_ref[0]) + shq_ref[0]).astype(BF16)
    hk = (xn * gk_ref[...] * (1.0 + sck_ref[0]) + shk_ref[0]).astype(BF16)
    q_ref[...] = (_dot(hq, wq_ref[...]) * (LOG2E * HEAD_DIM ** -0.5)).astype(BF16)
    k = _dot(hk, wkv_ref[:, :width])
    v = _dot(hk, wkv_ref[:, width:])
    k_ref[...] = k
    v_ref[...] = v
    kb_ref[...] = k.astype(BF16)
    vb_ref[...] = v.astype(BF16)


def _attn_in(x, gq, scq, shq, gk, sck, shk, wq, wkv, *, nseq, tm):
    t, d = x.shape
    width = wq.shape[1]
    nt = t // (nseq * tm)
    row = lambda b, i: (b * nt + i, 0)
    vec = _const_spec((1, d))
    blk = pl.BlockSpec((tm, width), row)
    return pl.pallas_call(
        functools.partial(_attn_in_kernel, width=width),
        grid=(nseq, nt),
        in_specs=[pl.BlockSpec((tm, d), row), vec, _mod_spec(scq, tm), _mod_spec(shq, tm),
                  vec, _mod_spec(sck, tm), _mod_spec(shk, tm),
                  _const_spec(wq.shape), _const_spec(wkv.shape)],
        out_specs=[blk] * 5,
        out_shape=[jax.ShapeDtypeStruct((t, width), BF16), jax.ShapeDtypeStruct((t, width), F32),
                   jax.ShapeDtypeStruct((t, width), F32), jax.ShapeDtypeStruct((t, width), BF16),
                   jax.ShapeDtypeStruct((t, width), BF16)],
        compiler_params=_params(("arbitrary", "arbitrary")),
        name="attn_in",
    )(x, gq, scq, shq, gk, sck, shk, wq, wkv)


def _causal_mask(shape):
    ri = lax.broadcasted_iota(jnp.int32, shape, 0)
    ci = lax.broadcasted_iota(jnp.int32, shape, 1)
    return ci < ri


def _sb_prompt_kernel(bias_ref, tri_ref, q_ref, k_ref, v_ref, o_ref, z_scr, acc_scr, carry_scr,
                      *, tq, hpb):
    hg = pl.program_id(1)
    i = pl.program_id(2)
    heads = range(hpb)
    hsl = [slice(j * HEAD_DIM, (j + 1) * HEAD_DIM) for j in heads]
    qs = [q_ref[:, hs] for hs in hsl]
    biases = [bias_ref[hg * hpb + j] * LOG2E for j in heads]

    def rows(b):
        return pl.ds(pl.multiple_of(jnp.maximum(i - b, 0) * tq, tq), tq)

    def logits(b, slot):
        r = rows(b)
        for j in heads:
            z_scr[slot, j] = _dot_nt(qs[j], k_ref[r, hsl[j]]) + biases[j]

    def consume(b, slot, first):
        r = rows(b)
        lsig, sp_bf = [], []
        for j in heads:
            z = z_scr[slot, j]
            sp = _softplus2(z)
            if first:
                mask = _causal_mask(z.shape)
                lsig.append(z - sp)
                sp = jnp.where(mask, sp, 0.0)
                carry_scr[j] = jnp.sum(sp, axis=-1, keepdims=True)
            else:
                carry = carry_scr[j]
                lsig.append(z - sp - carry)
                carry_scr[j] = carry + jnp.sum(sp, axis=-1, keepdims=True)
            sp_bf.append(sp.astype(BF16))
        att = [jnp.exp2(lsig[j] - _dot(sp_bf[j], tri_ref[...])) for j in heads]
        if first:
            att = [jnp.where(mask, att[j], 0.0) for j in heads]
        for j in heads:
            contrib = _dot(att[j].astype(BF16), v_ref[r, hsl[j]])
            acc_scr[j] = contrib if first else acc_scr[j] + contrib

    logits(0, 0)
    logits(1, 1)
    consume(0, 0, True)

    def pair(p, c):
        b = 2 * p + 1
        logits(b + 1, 0)
        consume(b, 1, False)
        logits(b + 2, 1)
        consume(b + 1, 0, False)
        return c

    lax.fori_loop(0, i // 2, pair, 0)

    @pl.when(i % 2 == 1)
    def _():
        consume(i, 1, False)

    for j in heads:
        o_ref[:, hsl[j]] = acc_scr[j].astype(o_ref.dtype)


def _sb_prompt(q, kb, vb, bias, *, nseq, tq, hpb):
    t, width = q.shape
    nh = width // HEAD_DIM
    seq = t // nseq
    nq = seq // tq
    ri = jnp.arange(tq)[:, None]
    ci = jnp.arange(tq)[None, :]
    tri = (ri > ci).astype(BF16)
    wblk = hpb * HEAD_DIM
    return pl.pallas_call(
        functools.partial(_sb_prompt_kernel, tq=tq, hpb=hpb),
        grid=(nseq, nh // hpb, nq),
        in_specs=[pl.BlockSpec(memory_space=pltpu.SMEM),
                  _const_spec((tq, tq)),
                  pl.BlockSpec((tq, wblk), lambda b, h, i: (b * nq + i, h)),
                  pl.BlockSpec((seq, wblk), lambda b, h, i: (b, h)),
                  pl.BlockSpec((seq, wblk), lambda b, h, i: (b, h))],
        out_specs=pl.BlockSpec((tq, wblk), lambda b, h, i: (b * nq + i, h)),
        out_shape=jax.ShapeDtypeStruct((t, width), BF16),
        scratch_shapes=[pltpu.VMEM((2, hpb, tq, tq), F32), pltpu.VMEM((hpb, tq, HEAD_DIM), F32),
                        pltpu.VMEM((hpb, tq, 1), F32)],
        compiler_params=_params(("arbitrary", "arbitrary", "arbitrary")),
        name="sb_prompt",
    )(bias, tri, q, kb, vb)


def _dec_blocks(kcats, vcats, qbd, bias, carry, keymask):
    nb = range(len(kcats))
    z = [_dot_nt(qbd, kcats[p]) + bias for p in nb]
    sp = [_softplus2(z[p]) for p in nb]
    if keymask is not None:
        sp = [jnp.where(keymask, sp[p], 0.0) for p in nb]
    n = z[0].shape[1]
    ri = lax.broadcasted_iota(jnp.int32, (n, n), 0)
    ci = lax.broadcasted_iota(jnp.int32, (n, n), 1)
    mrev = (ri > ci).astype(BF16)
    within = [_dot(sp[p].astype(BF16), mrev) for p in nb]
    out = None
    for p in nb:
        att = jnp.exp2(z[p] - sp[p] - within[p] - carry)
        if keymask is not None:
            att = jnp.where(keymask, att, 0.0)
        contrib = _dot(att.astype(BF16), vcats[p])
        out = contrib if out is None else out + contrib
        carry = carry + jnp.sum(sp[p], axis=1, keepdims=True)
    return out, carry


def _sb_decode_kernel(pt_ref, qbd_ref, bias_ref, knew_ref, vnew_ref, *rest, pages_per_step, nh):
    del pt_ref
    pp = pages_per_step
    k_refs = rest[:pp]
    v_refs = rest[pp:2 * pp]
    o_ref, acc_ref, carry_ref = rest[2 * pp:]
    j = pl.program_id(1)
    qbd = qbd_ref[0]
    bias = bias_ref[...] * LOG2E
    w = qbd.shape[0]

    @pl.when(j == 0)
    def _():
        s_idx = lax.broadcasted_iota(jnp.int32, (w, ROW_GROUP), 1)
        i_idx = lax.broadcasted_iota(jnp.int32, (w, ROW_GROUP), 0) % ROW_GROUP
        real = jnp.logical_and(s_idx >= HIST_ROWS, s_idx < ROW_GROUP - 1)
        keymask = jnp.logical_and(real, s_idx < i_idx)
        contrib, carry = _dec_blocks([knew_ref[0].astype(BF16)], [vnew_ref[0].astype(BF16)], qbd, bias,
                                     jnp.zeros((w, 1), F32), keymask)
        acc_ref[...] = contrib
        carry_ref[...] = jnp.broadcast_to(carry, carry_ref.shape)

    n_tok = k_refs[0].shape[0] // nh

    def cat(ref):
        return jnp.concatenate([ref[pl.ds(h, n_tok, stride=nh), :].astype(BF16) for h in range(nh)], axis=1)

    contrib, carry = _dec_blocks([cat(r) for r in k_refs], [cat(r) for r in v_refs], qbd, bias,
                                 carry_ref[:, 0:1], None)
    acc = acc_ref[...] + contrib
    acc_ref[...] = acc
    carry_ref[...] = jnp.broadcast_to(carry, carry_ref.shape)

    @pl.when(j == pl.num_programs(1) - 1)
    def _():
        o_ref[0] = jnp.concatenate(
            [acc[h * ROW_GROUP:(h + 1) * ROW_GROUP, h * HEAD_DIM:(h + 1) * HEAD_DIM] for h in range(nh)],
            axis=0).astype(o_ref.dtype)


def _sb_decode(qbd, bias, knew, vnew, cache_k2, cache_v2, page_table, *, n_pool, pages_per_step):
    nb, n_pages = page_table.shape
    pp = pages_per_step
    steps = n_pages // pp
    w, width = qbd.shape[1:]
    nh = width // HEAD_DIM
    page_rows = cache_k2.shape[0] // n_pool

    def page_spec(r):
        def imap(b, j, pt):
            return (pt[b * n_pages + (n_pages - 1 - (j * pp + r))], 0)
        return pl.BlockSpec((page_rows, HEAD_DIM), imap)

    seq3 = lambda b, j, pt: (b, 0, 0)
    grid_spec = pltpu.PrefetchScalarGridSpec(
        num_scalar_prefetch=1,
        grid=(nb, steps),
        in_specs=[pl.BlockSpec((1, w, width), seq3),
                  pl.BlockSpec((w, 1), lambda b, j, pt: (0, 0)),
                  pl.BlockSpec((1, ROW_GROUP, width), seq3),
                  pl.BlockSpec((1, ROW_GROUP, width), seq3)]
        + [page_spec(r) for r in range(pp)] * 2,
        out_specs=pl.BlockSpec((1, w, HEAD_DIM), seq3),
        scratch_shapes=[pltpu.VMEM((w, width), F32), pltpu.VMEM((w, HEAD_DIM), F32)],
    )
    return pl.pallas_call(
        functools.partial(_sb_decode_kernel, pages_per_step=pp, nh=nh),
        grid_spec=grid_spec,
        out_shape=jax.ShapeDtypeStruct((nb, w, HEAD_DIM), BF16),
        compiler_params=_params(("arbitrary", "arbitrary")),
        name="sb_decode",
    )(page_table.reshape(-1), qbd, bias, knew, vnew,
      *([cache_k2] * pp), *([cache_v2] * pp))


def _mods(m, n, rows):
    parts = jnp.split(m, n, axis=-1)
    if rows is None:
        return [p[:, None, :] for p in parts]
    return [jnp.repeat(p, rows, axis=0)[None] for p in parts]


def _trunk(x2d, m0, m1, mkv, wts, *, nseq, tm, ffn_tm, rows_per_mod, dn_nseq, dn_tm, dn_chunk, s0, inj,
           attend):
    sh1, sc1, gt1, sh2, sc2, gt2 = _mods(m0, 6, rows_per_mod)
    dn = _dn_in(x2d, sc1, sh1, wts["mix_pre_g"][0], wts["dn_w_qkvz"], wts["dn_w_bg"],
                wts["dn_conv_w"], wts["dn_ap"], inj, nseq=nseq, tm=tm)
    q, k, v, z, bg, hist = dn[:6]
    xx = dn[6] if inj is not None else None
    o_dn, s_new = _delta(q, k, v, z, bg, s0, wts["dn_onorm_g"], nseq=dn_nseq, tm=dn_tm,
                         chunk=dn_chunk)
    x2 = _mixffn(o_dn, x2d, wts["dn_w_out"], wts["mix_post_g"][0], gt1, wts["ffn_pre_g"][0], sc2, sh2,
                 wts["ffn_w_in"][0], wts["ffn_w_out"][0], wts["ffn_post_g"][0], gt2, nseq=nseq, tm=ffn_tm)

    sh1, sc1, gt1, sh2, sc2, gt2 = _mods(m1, 6, rows_per_mod)
    ksh, ksc = _mods(mkv, 2, rows_per_mod)
    qb, k_new, v_new, kb, vb = _attn_in(x2, wts["mix_pre_g"][1], sc1, sh1, wts["kv_norm_g"], ksc, ksh,
                                        wts["sb_w_q"], wts["sb_w_kv"], nseq=nseq, tm=tm)
    o_sb = attend(qb, k_new, v_new, kb, vb)
    y = _mixffn(o_sb, x2, wts["sb_w_o"], wts["mix_post_g"][1], gt1, wts["ffn_pre_g"][1], sc2, sh2,
                wts["ffn_w_in"][1], wts["ffn_w_out"][1], wts["ffn_post_g"][1], gt2, nseq=nseq, tm=ffn_tm)
    return y, hist, xx, s_new, k_new, v_new


def kernel(x_prompt, x_sample, c_prompt, c_sample, state_dn_conv, state_dn_S, cache_k, cache_v, page_table, ada_w, ada_b, mix_pre_g, mix_post_g, ffn_pre_g, ffn_post_g, ffn_w_in, ffn_w_out, dn_w_in, dn_conv_w, dn_a_log, dn_dt_bias, dn_onorm_g, dn_w_out, kv_ada_w, kv_ada_b, kv_norm_g, sb_w_kv, sb_w_q, sb_logit_bias, sb_w_o):
    bp, seq, d = x_prompt.shape
    db, dseq, _ = x_sample.shape
    nh = d // HEAD_DIM
    width = nh * HEAD_DIM
    assert ada_w.shape[0] == 2 and dn_w_in.shape[0] == 1 and sb_w_q.shape[0] == 1
    assert dseq == ROW_GROUP - HIST_ROWS - 1 and cache_k.shape[1] == HEAD_DIM and cache_k.shape[2] == nh

    vec = lambda g: g.reshape(g.shape[0], 1, g.shape[1])
    w_in = dn_w_in[0]
    pad_lanes = HEAD_DIM - 2 * nh
    zpad = jnp.zeros((nh,), F32)
    lane_pad = lambda a: jnp.concatenate([zpad, a, jnp.zeros((pad_lanes,), F32)])
    wts = {
        "mix_pre_g": vec(mix_pre_g), "mix_post_g": vec(mix_post_g),
        "ffn_pre_g": vec(ffn_pre_g), "ffn_post_g": vec(ffn_post_g),
        "ffn_w_in": [w.astype(BF16) for w in ffn_w_in], "ffn_w_out": [w.astype(BF16) for w in ffn_w_out],
        "dn_w_qkvz": w_in[:, :4 * width].astype(BF16),
        "dn_w_bg": jnp.pad(w_in[:, 4 * width:], ((0, 0), (0, pad_lanes))).astype(BF16),
        "dn_conv_w": dn_conv_w[0],
        "dn_ap": jnp.stack([lane_pad(-jnp.exp(dn_a_log[0])), lane_pad(dn_dt_bias[0])]),
        "dn_onorm_g": dn_onorm_g,
        "dn_w_out": dn_w_out[0].astype(BF16),
        "kv_norm_g": kv_norm_g.reshape(1, d),
        "sb_w_kv": sb_w_kv.astype(BF16), "sb_w_q": sb_w_q[0].astype(BF16),
        "sb_w_o": sb_w_o[0].astype(BF16),
    }

    c_all = jnp.concatenate([c_prompt, c_sample], axis=0)
    c_all = jnp.pad(c_all, ((0, -(bp + db) % 16), (0, 0)))
    m0 = _ada(c_all, ada_w[0], ada_b[0])
    m1 = _ada(c_all, ada_w[1], ada_b[1])
    mkv = _ada(c_all, kv_ada_w, kv_ada_b)

    tm_p = min(ROW_TILE, seq)
    chunk_p = min(DN_CHUNK, seq)
    dn_tm_p = min(DN_ROW_TILE, seq)
    tq = min(SB_BLOCK, seq)
    hpb = min(SB_HEADS_PER_STEP, nh)
    bias_p = sb_logit_bias[0]

    def attend_prompt(qb, k_new, v_new, kb, vb):
        return _sb_prompt(qb, kb, vb, bias_p, nseq=bp, tq=tq, hpb=hpb)

    y_p, hist_p, _, s_p, k_p, v_p = _trunk(
        x_prompt.reshape(bp * seq, d), m0[:bp], m1[:bp], mkv[:bp], wts,
        nseq=bp, tm=tm_p, ffn_tm=min(FFN_ROW_TILE, seq), rows_per_mod=None, dn_nseq=bp, dn_tm=dn_tm_p, dn_chunk=chunk_p,
        s0=jnp.zeros((bp, nh, HEAD_DIM, HEAD_DIM), F32), inj=None, attend=attend_prompt)

    rows_s = db * ROW_GROUP
    pad_t = ROW_GROUP - HIST_ROWS - dseq
    xs = jnp.pad(x_sample, ((0, 0), (HIST_ROWS, pad_t), (0, 0))).reshape(rows_s, d)
    inj = jnp.pad(state_dn_conv[0], ((0, 0), (0, ROW_GROUP - HIST_ROWS), (0, 0))).reshape(rows_s, -1)
    n_pool = cache_k.shape[0]
    ck2 = cache_k.reshape(n_pool * HEAD_DIM * nh, HEAD_DIM)
    cv2 = cache_v.reshape(n_pool * HEAD_DIM * nh, HEAD_DIM)
    bias_s = jnp.repeat(sb_logit_bias[0], ROW_GROUP)[:, None]
    n_pages = page_table.shape[1]
    pps = DECODE_PAGES_PER_STEP if n_pages % DECODE_PAGES_PER_STEP == 0 else 1

    def attend_sample(qb, k_new, v_new, kb, vb):
        qt = qb.reshape(db, ROW_GROUP, nh, HEAD_DIM).transpose(0, 2, 1, 3)
        qbd = (qt[:, :, :, None, :] * jnp.eye(nh, dtype=BF16)[None, :, None, :, None]).reshape(
            db, nh * ROW_GROUP, width)
        knew = k_new.reshape(db, ROW_GROUP, width)
        vnew = v_new.reshape(db, ROW_GROUP, width)
        o = _sb_decode(qbd, bias_s, knew, vnew, ck2, cv2, page_table, n_pool=n_pool, pages_per_step=pps)
        return o.reshape(db, nh, ROW_GROUP, HEAD_DIM).transpose(0, 2, 1, 3).reshape(rows_s, width)

    tm_s = min(ROW_TILE, rows_s)
    y_s, _, xx_s, s_s, k_s, v_s = _trunk(
        xs, m0[bp:bp + db], m1[bp:bp + db], mkv[bp:bp + db], wts,
        nseq=1, tm=tm_s, ffn_tm=tm_s, rows_per_mod=ROW_GROUP, dn_nseq=db, dn_tm=ROW_GROUP, dn_chunk=ROW_GROUP,
        s0=state_dn_S[0], inj=inj, attend=attend_sample)

    real = slice(HIST_ROWS, HIST_ROWS + dseq)
    grp = lambda a: a.reshape((db, ROW_GROUP) + a.shape[1:])
    return (
        y_p.reshape(bp, seq, d),
        grp(y_s)[:, real],
        hist_p[None, :, ROW_GROUP - HIST_ROWS:],
        s_p[None],
        k_p.reshape(bp, seq, nh, HEAD_DIM),
        v_p.reshape(bp, seq, nh, HEAD_DIM),
        grp(xx_s)[None, :, dseq:dseq + HIST_ROWS],
        s_s[None],
        grp(k_s)[:, real].reshape(db, dseq, nh, HEAD_DIM),
        grp(v_s)[:, real].reshape(db, dseq, nh, HEAD_DIM),
    )
```
